```python
import jax, jax.numpy as jnp
from jax import lax
import numpy as np

D_MODEL = 1024
BATCH = 16
SEQ = 2048
DEPTH = 1

MEM_LEN = 256
HEAD_DIM = 64
BLOCK_Q = 128
ROPE_THETA = 10000.0
EPS = 1e-6
DIL_PAIRS = ((128, 1), (512, 4), (2048, 16))
N_DIL_GROUPS = 3
DIL_HEADS = 8
DIL_WIDTH = DIL_HEADS * HEAD_DIM
SB_HEADS = 8
SB_WIDTH = SB_HEADS * HEAD_DIM
MEM_HEADS = 4
MEM_HEAD_DIM = 128
MEM_WIDTH = MEM_HEADS * MEM_HEAD_DIM
N_BRANCHES = 3
A_QKV_COLS = 3 * N_DIL_GROUPS * DIL_WIDTH
B_QKV_COLS = 3 * SB_WIDTH
M_Q_COLS = MEM_WIDTH
GATE_COLS = N_BRANCHES * D_MODEL
OFF_B = A_QKV_COLS
OFF_M = OFF_B + B_QKV_COLS
OFF_G = OFF_M + M_Q_COLS
IN_COLS = OFF_G + GATE_COLS
N_GROUPS = 4
EXPERTS_PER_GROUP = 4
N_EXPERTS = N_GROUPS * EXPERTS_PER_GROUP
TOP_K_IN_GROUP = 2
D_EXPERT = 512

kernel_name = 'hybrid_dilated_stickbreak_memory_hmoe'


def _rmsnorm(x, gain):
    xf = x.astype(jnp.float32)
    y = xf * lax.rsqrt(jnp.mean(xf * xf, axis=-1, keepdims=True) + EPS)
    return (y * gain.astype(jnp.float32)).astype(x.dtype)


def _split_heads(t, n_heads, head_dim):
    b, s, _ = t.shape
    return t.reshape(b, s, n_heads, head_dim).transpose(0, 2, 1, 3)


def _merge_heads(t):
    b, h, s, d = t.shape
    return t.transpose(0, 2, 1, 3).reshape(b, s, h * d)


def _rope(t, positions):
    dh = t.shape[-1]
    inv_freq = ROPE_THETA ** (-jnp.arange(0, dh, 2, dtype=jnp.float32) / dh)
    ang = positions.astype(jnp.float32)[:, None, :, None] * inv_freq
    cos, sin = jnp.cos(ang), jnp.sin(ang)
    tf = t.astype(jnp.float32)
    t1, t2 = tf[..., : dh // 2], tf[..., dh // 2:]
    return jnp.concatenate([t1 * cos - t2 * sin, t2 * cos + t1 * sin], axis=-1).astype(t.dtype)


def _banded_attention(q, k, v, span):
    b, h, n, dh = q.shape
    nb = -(-n // BLOCK_Q)
    pad = nb * BLOCK_Q - n
    qb = jnp.pad(q, ((0, 0), (0, 0), (0, pad), (0, 0))).reshape(b, h, nb, BLOCK_Q, dh)

    def key_blocks(t):
        tb = jnp.pad(t, ((0, 0), (0, 0), (BLOCK_Q, pad), (0, 0))).reshape(b, h, nb + 1, BLOCK_Q, dh)
        return jnp.concatenate([tb[:, :, :-1], tb[:, :, 1:]], axis=3)

    kb, vb = key_blocks(k), key_blocks(v)
    scores = jnp.einsum('bhnqd,bhnkd->bhnqk', qb, kb, preferred_element_type=jnp.float32) * (dh ** -0.5)
    qi = jnp.arange(BLOCK_Q)[:, None]
    kj = jnp.arange(2 * BLOCK_Q)[None, :]
    dist = qi + BLOCK_Q - kj
    kpos = jnp.arange(nb)[:, None, None] * BLOCK_Q - BLOCK_Q + kj[None]
    mask = (dist >= 0) & (dist <= span) & (kpos >= 0)
    scores = jnp.where(mask, scores, -jnp.inf)
    m = jnp.max(scores, axis=-1, keepdims=True)
    p = jnp.exp(scores - m)
    den = jnp.sum(p, axis=-1, keepdims=True)
    out = jnp.einsum('bhnqk,bhnkd->bhnqd', p, vb.astype(jnp.float32)) / den
    lse = (m + jnp.log(den))[..., 0]
    out = out.reshape(b, h, nb * BLOCK_Q, dh)[:, :, :n]
    lse = lse.reshape(b, h, nb * BLOCK_Q)[:, :, :n]
    return out, lse


def _dilated_group(q, k, v, window, dilation):
    b, h, s, dh = q.shape
    span = window // dilation
    sub = s // dilation

    def to_sub(t):
        return t.reshape(b, h, sub, dilation, dh).transpose(0, 1, 3, 2, 4).reshape(b, h * dilation, sub, dh)

    o, lse = _banded_attention(to_sub(q), to_sub(k), to_sub(v), span)
    o = o.reshape(b, h, dilation, sub, dh).transpose(0, 1, 3, 2, 4).reshape(b, h, s, dh)
    lse = lse.reshape(b, h, dilation, sub).transpose(0, 1, 3, 2).reshape(b, h, s)
    return o, lse


def _dilated_mixer(cols, positions, qn, kn):
    b, s, _ = cols.shape
    qkv = cols.reshape(b, s, 3, N_DIL_GROUPS, DIL_HEADS, HEAD_DIM)
    outs, lses = [], []
    for g, (window, dilation) in enumerate(DIL_PAIRS):
        q = _rope(_rmsnorm(qkv[:, :, 0, g].transpose(0, 2, 1, 3), qn), positions)
        k = _rope(_rmsnorm(qkv[:, :, 1, g].transpose(0, 2, 1, 3), kn), positions)
        v = qkv[:, :, 2, g].transpose(0, 2, 1, 3)
        o, lse = _dilated_group(q, k, v, window, dilation)
        outs.append(o)
        lses.append(lse)
    wts = jax.nn.softmax(jnp.stack(lses, axis=0), axis=0)
    o = wts[0][..., None] * outs[0] + wts[1][..., None] * outs[1] + wts[2][..., None] * outs[2]
    return _merge_heads(o.astype(cols.dtype))


def _stick_breaking_mixer(cols):
    b, s, _ = cols.shape
    qkv = cols.reshape(b, s, 3, SB_HEADS, HEAD_DIM)
    q = qkv[:, :, 0].transpose(0, 2, 1, 3)
    k = qkv[:, :, 1].transpose(0, 2, 1, 3)
    v = qkv[:, :, 2].transpose(0, 2, 1, 3)
    scale = HEAD_DIM ** -0.5
    outs = []
    for n in range(s // BLOCK_Q):
        start, end = n * BLOCK_Q, (n + 1) * BLOCK_Q
        z = jnp.einsum('bhqd,bhkd->bhqk', q[:, :, start:end], k[:, :, :end],
                       preferred_element_type=jnp.float32) * scale
        tpos = start + jnp.arange(BLOCK_Q)[:, None]
        spos = jnp.arange(end)[None, :]
        causal = spos < tpos
        log_beta = jax.nn.log_sigmoid(z)
        log_1m = jnp.where(causal, jax.nn.log_sigmoid(-z), 0.0)
        suffix = lax.cumsum(log_1m, axis=3, reverse=True) - log_1m
        a = jnp.where(causal, jnp.exp(log_beta + suffix), 0.0)
        outs.append(jnp.einsum('bhqk,bhkd->bhqd', a, v[:, :, :end].astype(jnp.float32)))
    o = jnp.concatenate(outs, axis=2)
    return _merge_heads(o.astype(cols.dtype))


def _memory_mixer(cols, mem, norm_mem, w_mem_kv, qn, kn):
    q = _rmsnorm(_split_heads(cols, MEM_HEADS, MEM_HEAD_DIM), qn)
    kv = _rmsnorm(mem, norm_mem) @ w_mem_kv
    k = _rmsnorm(_split_heads(kv[..., :MEM_WIDTH], MEM_HEADS, MEM_HEAD_DIM), kn)
    v = _split_heads(kv[..., MEM_WIDTH:], MEM_HEADS, MEM_HEAD_DIM)
    scores = jnp.einsum('bhsd,bhmd->bhsm', q, k, preferred_element_type=jnp.float32) * (MEM_HEAD_DIM ** -0.5)
    p = jax.nn.softmax(scores, axis=-1)
    o = jnp.einsum('bhsm,bhmd->bhsd', p, v.astype(jnp.float32))
    return _merge_heads(o.astype(cols.dtype))


def _hier_moe(h, w_router_group, b_router_group, w_router_expert, b_router_expert,
              w_exp_gate, w_exp_up, w_exp_down):
    b, s, d = h.shape
    hf = h.reshape(b * s, d)
    n_tok = hf.shape[0]
    g_logits = (hf @ w_router_group).astype(jnp.float32) + b_router_group.astype(jnp.float32)
    g_prob = jax.nn.softmax(g_logits, axis=-1)
    g_top, g_idx = lax.top_k(g_prob, 1)
    e_logits = ((hf @ w_router_expert).astype(jnp.float32)
                + b_router_expert.astype(jnp.float32)).reshape(n_tok, N_GROUPS, EXPERTS_PER_GROUP)
    e_sel = jnp.einsum('nge,ng->ne', e_logits, jax.nn.one_hot(g_idx[:, 0], N_GROUPS, dtype=jnp.float32))
    e_prob = jax.nn.softmax(e_sel, axis=-1)
    e_top, e_idx = lax.top_k(e_prob, TOP_K_IN_GROUP)
    e_w = e_top / jnp.sum(e_top, axis=-1, keepdims=True) * g_top
    expert_id = g_idx * EXPERTS_PER_GROUP + e_idx
    combine = jnp.einsum('nk,nke->ne', e_w, jax.nn.one_hot(expert_id, N_EXPERTS, dtype=jnp.float32))
    out = jnp.zeros((n_tok, d), jnp.float32)
    for e in range(N_EXPERTS):
        y = (jax.nn.silu(hf @ w_exp_gate[e]) * (hf @ w_exp_up[e])) @ w_exp_down[e]
        out = out + combine[:, e:e + 1] * y.astype(jnp.float32)
    return out.astype(h.dtype).reshape(b, s, d)


def setup_inputs(seed: int = 0) -> dict:
    key = jax.random.key(seed)
    ks = jax.random.split(key, 24)
    f32 = jnp.float32
    L = DEPTH

    def nrm(k, shape, fan_in):
        return jax.random.normal(k, shape, f32) * (fan_in ** -0.5)

    def gain(k, shape):
        return 1.0 + 0.02 * jax.random.normal(k, shape, f32)

    return {
        'x': jax.random.normal(ks[0], (BATCH, SEQ, D_MODEL), f32),
        'mem': jax.random.normal(ks[1], (BATCH, MEM_LEN, D_MODEL), f32),
        'positions': jnp.tile(jnp.arange(SEQ, dtype=jnp.int32)[None, :], (BATCH, 1)),
        'norm_mix': gain(ks[2], (L, D_MODEL)),
        'norm_mem': gain(ks[3], (L, D_MODEL)),
        'w_in': nrm(ks[4], (L, D_MODEL, IN_COLS), D_MODEL),
        'b_gate': 0.02 * jax.random.normal(ks[5], (L, GATE_COLS), f32),
        'qn_dil': gain(ks[6], (L, HEAD_DIM)),
        'kn_dil': gain(ks[7], (L, HEAD_DIM)),
        'qn_mem': gain(ks[8], (L, MEM_HEAD_DIM)),
        'kn_mem': gain(ks[9], (L, MEM_HEAD_DIM)),
        'w_mem_kv': nrm(ks[10], (L, D_MODEL, 2 * MEM_WIDTH), D_MODEL),
        'w_o_dil': nrm(ks[11], (L, DIL_WIDTH, D_MODEL), DIL_WIDTH),
        'w_o_sb': nrm(ks[12], (L, SB_WIDTH, D_MODEL), SB_WIDTH),
        'w_o_mem': nrm(ks[13], (L, MEM_WIDTH, D_MODEL), MEM_WIDTH),
        'w_out': nrm(ks[14], (L, D_MODEL, D_MODEL), D_MODEL),
        'norm_ffn': gain(ks[15], (L, D_MODEL)),
        'w_router_group': nrm(ks[16], (L, D_MODEL, N_GROUPS), D_MODEL),
        'b_router_group': 0.01 * jax.random.normal(ks[17], (L, N_GROUPS), f32),
        'w_router_expert': nrm(ks[18], (L, D_MODEL, N_EXPERTS), D_MODEL),
        'b_router_expert': 0.01 * jax.random.normal(ks[19], (L, N_EXPERTS), f32),
        'w_exp_gate': nrm(ks[20], (L, N_EXPERTS, D_MODEL, D_EXPERT), D_MODEL),
        'w_exp_up': nrm(ks[21], (L, N_EXPERTS, D_MODEL, D_EXPERT), D_MODEL),
        'w_exp_down': nrm(ks[22], (L, N_EXPERTS, D_EXPERT, D_MODEL), D_EXPERT),
    }


def reference(x, mem, positions, norm_mix, norm_mem, w_in, b_gate, qn_dil, kn_dil, qn_mem, kn_mem,
              w_mem_kv, w_o_dil, w_o_sb, w_o_mem, w_out, norm_ffn, w_router_group, b_router_group,
              w_router_expert, b_router_expert, w_exp_gate, w_exp_up, w_exp_down):
    b, s, _ = x.shape
    for l in range(DEPTH):
        h = _rmsnorm(x, norm_mix[l])
        proj = h @ w_in[l]
        o_dil = _dilated_mixer(proj[..., :OFF_B], positions, qn_dil[l], kn_dil[l])
        o_sb = _stick_breaking_mixer(proj[..., OFF_B:OFF_M])
        o_mem = _memory_mixer(proj[..., OFF_M:OFF_G], mem, norm_mem[l], w_mem_kv[l], qn_mem[l], kn_mem[l])
        gates = jax.nn.sigmoid(proj[..., OFF_G:] + b_gate[l]).reshape(b, s, N_BRANCHES, D_MODEL)
        merged = (gates[:, :, 0] * (o_dil @ w_o_dil[l])
                  + gates[:, :, 1] * (o_sb @ w_o_sb[l])
                  + gates[:, :, 2] * (o_mem @ w_o_mem[l]))
        x = x + merged @ w_out[l]
        x = x + _hier_moe(_rmsnorm(x, norm_ffn[l]), w_router_group[l], b_router_group[l],
                          w_router_expert[l], b_router_expert[l],
                          w_exp_gate[l], w_exp_up[l], w_exp_down[l])
    return x
```

```python
import functools

import jax
import jax.numpy as jnp
from jax import lax
from jax.experimental import pallas as pl
from jax.experimental.pallas import tpu as pltpu

F32 = jnp.float32
BF16 = jnp.bfloat16

D_MODEL = 1024
HEAD_DIM = 64
LANES = 128
BLOCK_Q = 128
ROPE_THETA = 10000.0
EPS = 1e-6
DILATIONS = (1, 4, 16)
DIL_WIDTH = 512
SB_WIDTH = 512
MEM_WIDTH = 512
MEM_HEADS = 4
MEM_HEAD_DIM = 128
MEM_LEN = 256
GATE_COLS = 3 * D_MODEL
A_COLS = 9 * DIL_WIDTH
B_COLS = 3 * SB_WIDTH
IN_COLS = A_COLS + B_COLS + MEM_WIDTH + GATE_COLS
CB_DIL = GATE_COLS // LANES
CB_SB = CB_DIL + A_COLS // LANES
CB_MEM = CB_SB + B_COLS // LANES
N_GROUPS = 4
EXPERTS_PER_GROUP = 4
N_EXPERTS = 16
D_EXPERT = 512
N_PAIRS = 6
N_CLASSES = N_GROUPS * N_PAIRS
MOE_TILE = 256
NEG = -1e30
VMEM_LIMIT = 56 * 1024 * 1024


def _cparams(sem):
    return pltpu.CompilerParams(dimension_semantics=sem, vmem_limit_bytes=VMEM_LIMIT)


def _dot(a, b):
    return jnp.dot(a, b, preferred_element_type=F32)


def _dot_nt(a, b):
    return lax.dot_general(a, b, (((1,), (1,)), ((), ())), preferred_element_type=F32)


def _split_dot(a, b):
    hi = a.astype(BF16)
    lo = (a - hi.astype(F32)).astype(BF16)
    return _dot(hi, b) + _dot(lo, b)


def _inproj_kernel(x_ref, g_ref, w_ref, o_ref, *, chunk):
    x = x_ref[...]
    ms = jnp.mean(x * x, axis=-1, keepdims=True)
    h = (x * lax.rsqrt(ms + EPS) * g_ref[...]).astype(BF16)
    for c in range(IN_COLS // chunk):
        sl = slice(c * chunk, (c + 1) * chunk)
        o_ref[:, sl] = _dot(h, w_ref[:, sl]).astype(BF16)


def _in_proj(x2d, gain, w_bf16, tm=256, chunk=512):
    n = x2d.shape[0]
    return pl.pallas_call(
        functools.partial(_inproj_kernel, chunk=chunk),
        out_shape=jax.ShapeDtypeStruct((n, IN_COLS), BF16),
        grid=(n // tm,),
        in_specs=[
            pl.BlockSpec((tm, D_MODEL), lambda i: (i, 0)),
            pl.BlockSpec((1, D_MODEL), lambda i: (0, 0)),
            pl.BlockSpec((D_MODEL, IN_COLS), lambda i: (0, 0), pipeline_mode=pl.Buffered(1)),
        ],
        out_specs=pl.BlockSpec((tm, IN_COLS), lambda i: (i, 0)),
        compiler_params=_cparams(("parallel",)),
        name="in_proj",
    )(x2d, gain, w_bf16)


def _rope_kernel(pos_ref, invf_ref, sign_ref, cos_ref, sin_ref):
    ang = pos_ref[...] * invf_ref[...]
    cos_ref[...] = jnp.cos(ang)
    sin_ref[...] = jnp.sin(ang) * sign_ref[...]


def _rope_tables(pos_b, invf, sign, tm=2048):
    n = pos_b.shape[0]
    row = pl.BlockSpec((tm, LANES), lambda i: (i, 0))
    vec = pl.BlockSpec((1, LANES), lambda i: (0, 0))
    return pl.pallas_call(
        _rope_kernel,
        out_shape=(jax.ShapeDtypeStruct((n, LANES), F32),) * 2,
        grid=(n // tm,),
        in_specs=[row, vec, vec],
        out_specs=(row, row),
        compiler_params=_cparams(("parallel",)),
        name="rope_tab",
    )(pos_b, invf, sign)


def _head_masks(rows):
    lane = lax.broadcasted_iota(jnp.int32, (rows, LANES), 1)
    return lane < HEAD_DIM


def _dil_kernel(q0, k0, v0, q1, k1, v1, q2, k2, v2, cos_ref, sin_ref, qn_ref, kn_ref, o_ref,
                qf, kf, vf, acc0, acc1, m0, m1, *, seq):
    qkv = ((q0, k0, v0), (q1, k1, v1), (q2, k2, v2))
    accs, ms = (acc0, acc1), (m0, m1)
    chunk = 256
    n_chunks = seq // chunk

    gi = lax.broadcasted_iota(jnp.int32, (LANES, LANES), 0) // HEAD_DIM
    gj = lax.broadcasted_iota(jnp.int32, (LANES, LANES), 1) // HEAD_DIM
    head_mean = jnp.where(gi == gj, 1.0 / HEAD_DIM, 0.0).astype(BF16)
    lane_c = lax.broadcasted_iota(jnp.int32, (chunk, LANES), 1)
    first_half = (lane_c % HEAD_DIM) < (HEAD_DIM // 2)

    qi = lax.broadcasted_iota(jnp.int32, (BLOCK_Q, BLOCK_Q), 0)
    kj = lax.broadcasted_iota(jnp.int32, (BLOCK_Q, BLOCK_Q), 1)
    bias_cur = jnp.where(kj <= qi, 0.0, NEG).astype(F32)
    bias_prev = jnp.where(kj >= qi, 0.0, NEG).astype(F32)
    bias_band = jnp.concatenate([bias_prev, bias_cur], axis=1)
    hq = _head_masks(BLOCK_Q)

    def init(i, _):
        rows = pl.ds(pl.multiple_of(i * chunk, chunk), chunk)
        for r in accs:
            r[rows, :] = jnp.zeros((chunk, LANES), F32)
        for r in ms:
            r[rows, :] = jnp.full((chunk, LANES), NEG, F32)
        return 0

    lax.fori_loop(0, n_chunks, init, 0)

    def norm_rope(t_ref, gain, rows, scale):
        t = t_ref[0, rows, :].astype(F32)
        msq = _split_dot(t * t, head_mean)
        tn = t * lax.rsqrt(msq + EPS) * gain
        rot = jnp.where(first_half, pltpu.roll(tn, LANES - HEAD_DIM // 2, 1), pltpu.roll(tn, HEAD_DIM // 2, 1))
        out = tn * cos_ref[0, rows, :] + rot * sin_ref[0, rows, :]
        return out * scale if scale != 1.0 else out

    def attend(q_rows, k_rows, bias):
        q = qf[q_rows, :].astype(BF16)
        k = kf[k_rows, :].astype(BF16)
        v = vf[k_rows, :].astype(BF16)
        nk = k.shape[0]
        hv = _head_masks(nk)
        for hh in range(2):
            sel_q = hq if hh == 0 else jnp.logical_not(hq)
            sel_v = hv if hh == 0 else jnp.logical_not(hv)
            qm = jnp.where(sel_q, q, jnp.zeros_like(q))
            vext = jnp.where(sel_v, v, jnp.ones_like(v))
            s = _dot_nt(qm, k) + bias
            m_old = ms[hh][q_rows, :]
            m_new = jnp.maximum(m_old, jnp.max(s, axis=-1, keepdims=True))
            alpha = jnp.exp(m_old - m_new)
            p = jnp.exp(s - jnp.concatenate([m_new] * (nk // LANES), axis=1))
            accs[hh][q_rows, :] = alpha * accs[hh][q_rows, :] + _dot(p.astype(BF16), vext)
            ms[hh][q_rows, :] = m_new

    for g, dil in enumerate(DILATIONS):
        q_ref, k_ref, v_ref = qkv[g]

        def prep(i, _):
            rows = pl.ds(pl.multiple_of(i * chunk, chunk), chunk)
            qf[rows, :] = norm_rope(q_ref, qn_ref[...], rows, HEAD_DIM ** -0.5)
            kf[rows, :] = norm_rope(k_ref, kn_ref[...], rows, 1.0)
            vf[rows, :] = v_ref[0, rows, :].astype(F32)
            return 0

        lax.fori_loop(0, n_chunks, prep, 0)

        sub = seq // dil
        nb = sub // BLOCK_Q
        if dil == 1:
            attend(pl.ds(0, BLOCK_Q), pl.ds(0, BLOCK_Q), bias_cur)

            def blk(n, _):
                q0_ = pl.multiple_of(n * BLOCK_Q, BLOCK_Q)
                attend(pl.ds(q0_, BLOCK_Q), pl.ds(q0_ - BLOCK_Q, 2 * BLOCK_Q), bias_band)
                return 0

            lax.fori_loop(1, nb, blk, 0)
        else:
            for c in range(dil):
                for n in range(nb):
                    q_rows = pl.ds(c + dil * BLOCK_Q * n, BLOCK_Q, stride=dil)
                    if n == 0:
                        attend(q_rows, q_rows, bias_cur)
                    else:
                        k_rows = pl.ds(c + dil * BLOCK_Q * (n - 1), 2 * BLOCK_Q, stride=dil)
                        attend(q_rows, k_rows, bias_band)

    hc = _head_masks(chunk)

    def fin(i, _):
        rows = pl.ds(pl.multiple_of(i * chunk, chunk), chunk)
        a0 = acc0[rows, :]
        a1 = acc1[rows, :]
        o = jnp.where(hc, a0 / pltpu.roll(a0, HEAD_DIM, 1), a1 / pltpu.roll(a1, HEAD_DIM, 1))
        o_ref[0, rows, :] = o.astype(BF16)
        return 0

    lax.fori_loop(0, n_chunks, fin, 0)


def _dilated(proj3, cos3, sin3, qn2, kn2):
    b, s, _ = proj3.shape
    specs = []
    for g in range(3):
        for t in range(3):
            cb = CB_DIL + (t * 3 + g) * 4
            specs.append(pl.BlockSpec((1, s, LANES), lambda bi, j, cb=cb: (bi, 0, cb + j)))
    tab = pl.BlockSpec((1, s, LANES), lambda bi, j: (bi, 0, 0))
    vec = pl.BlockSpec((1, LANES), lambda bi, j: (0, 0))
    return pl.pallas_call(
        functools.partial(_dil_kernel, seq=s),
        out_shape=jax.ShapeDtypeStruct((b, s, DIL_WIDTH), BF16),
        grid=(b, DIL_WIDTH // LANES),
        in_specs=specs + [tab, tab, vec, vec],
        out_specs=pl.BlockSpec((1, s, LANES), lambda bi, j: (bi, 0, j)),
        scratch_shapes=[pltpu.VMEM((s, LANES), F32)] * 7,
        compiler_params=_cparams(("parallel", "arbitrary")),
        name="dilated",
    )(*([proj3] * 9), cos3, sin3, qn2, kn2)


def _sb_kernel(q_ref, k_ref, v_ref, o_ref, *, seq):
    nb = seq // BLOCK_Q
    qi = lax.broadcasted_iota(jnp.int32, (BLOCK_Q, BLOCK_Q), 0)
    kj = lax.broadcasted_iota(jnp.int32, (BLOCK_Q, BLOCK_Q), 1)
    causal = kj < qi
    uj = lax.broadcasted_iota(jnp.int32, (BLOCK_Q, 2 * BLOCK_Q), 0)
    us = lax.broadcasted_iota(jnp.int32, (BLOCK_Q, 2 * BLOCK_Q), 1)
    suffix = jnp.where((uj > us) | (us >= BLOCK_Q), 1.0, 0.0).astype(BF16)
    h0 = _head_masks(BLOCK_Q)

    def step(qm, kb, vm, carry, diag):
        c, acc = carry
        z = _dot_nt(qm, kb)
        sp = jnp.log(1.0 + jnp.exp(-jnp.abs(z)))
        log_beta = jnp.minimum(z, 0.0) - sp
        log_1m = -jnp.maximum(z, 0.0) - sp
        if diag:
            log_1m = jnp.where(causal, log_1m, 0.0)
        su = _split_dot(log_1m, suffix)
        a = jnp.exp(log_beta + su[:, :BLOCK_Q] + c)
        if diag:
            a = jnp.where(causal, a, 0.0)
        return c + su[:, BLOCK_Q:], acc + _dot(a.astype(BF16), vm)

    def qblock(n, _):
        q0_ = pl.multiple_of(n * BLOCK_Q, BLOCK_Q)
        rows = pl.ds(q0_, BLOCK_Q)
        q = q_ref[0, rows, :] * jnp.asarray(HEAD_DIM ** -0.5, BF16)
        total = jnp.zeros((BLOCK_Q, LANES), F32)
        for hh in range(2):
            sel = h0 if hh == 0 else jnp.logical_not(h0)
            qm = jnp.where(sel, q, jnp.zeros_like(q))

            def vmask(r):
                v = v_ref[0, r, :]
                return jnp.where(sel, v, jnp.zeros_like(v))

            zero = jnp.zeros((BLOCK_Q, LANES), F32)
            carry = step(qm, k_ref[0, rows, :], vmask(rows), (zero, zero), True)

            def inner(i, carry):
                r = pl.ds(pl.multiple_of(q0_ - (i + 1) * BLOCK_Q, BLOCK_Q), BLOCK_Q)
                return step(qm, k_ref[0, r, :], vmask(r), carry, False)

            _, acc = lax.fori_loop(0, n, inner, carry)
            total = total + acc
        o_ref[0, rows, :] = total.astype(BF16)
        return 0

    lax.fori_loop(0, nb, qblock, 0)


def _stick_breaking(proj3):
    b, s, _ = proj3.shape
    specs = [pl.BlockSpec((1, s, LANES), lambda bi, j, cb=CB_SB + t * 4: (bi, 0, cb + j)) for t in range(3)]
    return pl.pallas_call(
        functools.partial(_sb_kernel, seq=s),
        out_shape=jax.ShapeDtypeStruct((b, s, SB_WIDTH), BF16),
        grid=(b, SB_WIDTH // LANES),
        in_specs=specs,
        out_specs=pl.BlockSpec((1, s, LANES), lambda bi, j: (bi, 0, j)),
        compiler_params=_cparams(("parallel", "arbitrary")),
        name="stickbrk",
    )(proj3, proj3, proj3)


def _memkv_kernel(mem_ref, g_ref, w_ref, kn_ref, k_ref, v_ref):
    x = mem_ref[0]
    ms = jnp.mean(x * x, axis=-1, keepdims=True)
    h = (x * lax.rsqrt(ms + EPS) * g_ref[...]).astype(BF16)
    kv = _dot(h, w_ref[...])
    for hd in range(MEM_HEADS):
        sl = slice(hd * MEM_HEAD_DIM, (hd + 1) * MEM_HEAD_DIM)
        kh = kv[:, sl]
        msk = jnp.mean(kh * kh, axis=-1, keepdims=True)
        k_ref[0, :, sl] = (kh * lax.rsqrt(msk + EPS) * kn_ref[...]).astype(BF16)
    v_ref[0] = kv[:, MEM_WIDTH:].astype(BF16)


def _mem_kv(mem, gain, w_bf16, kn):
    b = mem.shape[0]
    out = jax.ShapeDtypeStruct((b, MEM_LEN, MEM_WIDTH), BF16)
    blk = pl.BlockSpec((1, MEM_LEN, MEM_WIDTH), lambda i: (i, 0, 0))
    return pl.pallas_call(
        _memkv_kernel,
        out_shape=(out, out),
        grid=(b,),
        in_specs=[
            pl.BlockSpec((1, MEM_LEN, D_MODEL), lambda i: (i, 0, 0)),
            pl.BlockSpec((1, D_MODEL), lambda i: (0, 0)),
            pl.BlockSpec((D_MODEL, 2 * MEM_WIDTH), lambda i: (0, 0)),
            pl.BlockSpec((1, MEM_HEAD_DIM), lambda i: (0, 0)),
        ],
        out_specs=(blk, blk),
        compiler_params=_cparams(("parallel",)),
        name="mem_kv",
    )(mem, gain, w_bf16, kn)


def _memattn_kernel(q_ref, k_ref, v_ref, qn_ref, o_ref):
    q = q_ref[0].astype(F32)
    ms = jnp.mean(q * q, axis=-1, keepdims=True)
    qn = (q * lax.rsqrt(ms + EPS) * qn_ref[...]).astype(BF16)
    s = _dot_nt(qn, k_ref[0]) * (MEM_HEAD_DIM ** -0.5)
    m = jnp.max(s, axis=-1, keepdims=True)
    p = jnp.exp(s - m)
    den = jnp.sum(p, axis=-1, keepdims=True)
    o_ref[0] = (_dot(p.astype(BF16), v_ref[0]) / den).astype(BF16)


def _mem_attn(proj3, k, v, qn, tq=1024):
    b, s, _ = proj3.shape
    kv = pl.BlockSpec((1, MEM_LEN, MEM_HEAD_DIM), lambda bi, h, i: (bi, 0, h))
    return pl.pallas_call(
        _memattn_kernel,
        out_shape=jax.ShapeDtypeStruct((b, s, MEM_WIDTH), BF16),
        grid=(b, MEM_HEADS, s // tq),
        in_specs=[
            pl.BlockSpec((1, tq, MEM_HEAD_DIM), lambda bi, h, i: (bi, i, CB_MEM + h)),
            kv, kv,
            pl.BlockSpec((1, MEM_HEAD_DIM), lambda bi, h, i: (0, 0)),
        ],
        out_specs=pl.BlockSpec((1, tq, MEM_HEAD_DIM), lambda bi, h, i: (bi, i, h)),
        compiler_params=_cparams(("parallel", "parallel", "arbitrary")),
        name="mem_attn",
    )(proj3, k, v, qn)


def _merge_kernel(x_ref, gl_ref, bg_ref, od_ref, os_ref, om_ref, wd_ref, ws_ref, wm_ref, wo_ref,
                  nf_ref, wr_ref, br_ref, x2_ref, r_ref):
    merged = None
    for i, (o_ref, w_ref) in enumerate(((od_ref, wd_ref), (os_ref, ws_ref), (om_ref, wm_ref))):
        sl = slice(i * D_MODEL, (i + 1) * D_MODEL)
        gate = jax.nn.sigmoid(gl_ref[:, sl].astype(F32) + bg_ref[:, sl])
        term = gate * _dot(o_ref[...], w_ref[...])
        merged = term if merged is None else merged + term
    x2 = x_ref[...] + _dot(merged.astype(BF16), wo_ref[...])
    x2_ref[...] = x2

    ms = jnp.mean(x2 * x2, axis=-1, keepdims=True)
    hn = x2 * lax.rsqrt(ms + EPS) * nf_ref[...]
    lt = lax.dot_general(wr_ref[...], hn, (((1,), (1,)), ((), ())), precision=lax.Precision.HIGHEST,
                         preferred_element_type=F32) + br_ref[...]
    row = lambda i: lt[i:i + 1, :]
    gl = [row(i) for i in range(N_GROUPS)]
    gmax = functools.reduce(jnp.maximum, gl)
    g_top = 1.0 / functools.reduce(lambda a, c: a + c, [jnp.exp(v - gmax) for v in gl])
    g_idx = jnp.where(gl[0] == gmax, 0, jnp.where(gl[1] == gmax, 1, jnp.where(gl[2] == gmax, 2, 3)))
    el = []
    for e in range(EXPERTS_PER_GROUP):
        v = row(N_GROUPS + 3 * EXPERTS_PER_GROUP + e)
        for g in range(N_GROUPS - 2, -1, -1):
            v = jnp.where(g_idx == g, row(N_GROUPS + g * EXPERTS_PER_GROUP + e), v)
        el.append(v)
    emax = functools.reduce(jnp.maximum, el)
    a_idx = jnp.where(el[0] == emax, 0, jnp.where(el[1] == emax, 1, jnp.where(el[2] == emax, 2, 3)))
    rest = [jnp.where(a_idx == e, -jnp.inf, el[e]) for e in range(EXPERTS_PER_GROUP)]
    rmax = functools.reduce(jnp.maximum, rest)
    b_idx = jnp.where(rest[0] == rmax, 0, jnp.where(rest[1] == rmax, 1, jnp.where(rest[2] == rmax, 2, 3)))
    ratio = jnp.exp(rmax - emax)
    w_a = g_top / (1.0 + ratio)
    w_b = g_top * ratio / (1.0 + ratio)
    lo = jnp.minimum(a_idx, b_idx)
    hi = jnp.maximum(a_idx, b_idx)
    pair = jnp.where(lo == 0, hi - 1, jnp.where(lo == 1, hi + 1, 5))
    cls = g_idx * N_PAIRS + pair
    a_first = a_idx < b_idx
    r_ref[0:1, :] = jnp.where(a_first, w_a, w_b)
    r_ref[1:2, :] = jnp.where(a_first, w_b, w_a)
    r_ref[2:3, :] = cls.astype(F32)
    r_ref[3:8, :] = jnp.zeros((5, lt.shape[1]), F32)


def _merge(x2d, proj, b_gate, o_dil, o_sb, o_mem, w_o_dil, w_o_sb, w_o_mem, w_out, norm_ffn, w_rt, b_rt, tm=256):
    n = x2d.shape[0]
    row = lambda w: pl.BlockSpec((tm, w), lambda i: (i, 0))
    full = lambda a, bdim: pl.BlockSpec((a, bdim), lambda i: (0, 0))
    return pl.pallas_call(
        _merge_kernel,
        out_shape=(jax.ShapeDtypeStruct((n, D_MODEL), F32), jax.ShapeDtypeStruct((8, n), F32)),
        grid=(n // tm,),
        in_specs=[
            row(D_MODEL), row(GATE_COLS), full(1, GATE_COLS),
            row(DIL_WIDTH), row(SB_WIDTH), row(MEM_WIDTH),
            full(DIL_WIDTH, D_MODEL), full(SB_WIDTH, D_MODEL), full(MEM_WIDTH, D_MODEL), full(D_MODEL, D_MODEL),
            full(1, D_MODEL), full(32, D_MODEL), full(32, 1),
        ],
        out_specs=(row(D_MODEL), pl.BlockSpec((8, tm), lambda i: (0, i))),
        compiler_params=_cparams(("parallel",)),
        name="merge",
    )(x2d, proj, b_gate, o_dil, o_sb, o_mem, w_o_dil, w_o_sb, w_o_mem, w_out, norm_ffn, w_rt, b_rt)


def _moe_kernel(elo_ref, ehi_ref, nv_ref, src_ref,
                x_hbm, nf_ref, cw_ref, wgl_ref, wul_ref, wdl_ref, wgh_ref, wuh_ref, wdh_ref,
                out_hbm, xbuf, obuf, gsem, ssem, *, tile, n_tiles):
    i = pl.program_id(0)
    slot = i % 2

    def row_in(t, r, s):
        return pltpu.make_async_copy(x_hbm.at[pl.ds(src_ref[t * tile + r], 1), :],
                                     xbuf.at[s, pl.ds(r, 1), :], gsem.at[s])

    def row_out(t, r, s):
        return pltpu.make_async_copy(obuf.at[s, pl.ds(r, 1), :],
                                     out_hbm.at[pl.ds(src_ref[t * tile + r], 1), :], ssem.at[s])

    def gather(t, s):
        @pl.when(nv_ref[t] > 0)
        def _():
            def body(r, _):
                row_in(t, r, s).start()
                return 0
            lax.fori_loop(0, tile, body, 0)

    def drain_out(t, s):
        def body(r, _):
            row_out(t, r, s).wait()
            return 0
        lax.fori_loop(0, nv_ref[t], body, 0)

    @pl.when(i == 0)
    def _():
        gather(0, 0)

    @pl.when(i + 1 < n_tiles)
    def _():
        gather(i + 1, 1 - slot)

    @pl.when(i >= 2)
    def _():
        drain_out(i - 2, slot)

    nv = nv_ref[i]

    @pl.when(nv > 0)
    def _():
        def body(r, _):
            row_in(i, r, slot).wait()
            return 0
        lax.fori_loop(0, tile, body, 0)

        x = xbuf[slot]
        ms = jnp.mean(x * x, axis=-1, keepdims=True)
        hn = (x * lax.rsqrt(ms + EPS) * nf_ref[...]).astype(BF16)
        moe = None
        for c, (wg, wu, wd) in enumerate(((wgl_ref, wul_ref, wdl_ref), (wgh_ref, wuh_ref, wdh_ref))):
            gate = _dot(hn, wg[0])
            up = _dot(hn, wu[0])
            act = (gate * jax.nn.sigmoid(gate) * up).astype(BF16)
            term = cw_ref[:, c:c + 1] * _dot(act, wd[0])
            moe = term if moe is None else moe + term
        obuf[slot] = x + moe

        def sbody(r, _):
            row_out(i, r, slot).start()
            return 0
        lax.fori_loop(0, nv, sbody, 0)

    @pl.when(i == n_tiles - 1)
    def _():
        @pl.when(i >= 1)
        def _():
            drain_out(i - 1, 1 - slot)
        drain_out(i, slot)


def _moe(x2, norm_ffn, cw_sorted, wg, wu, wd, tile_elo, tile_ehi, tile_nv, src, tile=MOE_TILE):
    n = x2.shape[0]
    n_tiles = tile_nv.shape[0]
    w_in = lambda sel: pl.BlockSpec((1, D_MODEL, D_EXPERT), lambda i, elo, ehi, nv, s: ((elo, ehi)[sel][i], 0, 0))
    w_dn = lambda sel: pl.BlockSpec((1, D_EXPERT, D_MODEL), lambda i, elo, ehi, nv, s: ((elo, ehi)[sel][i], 0, 0))
    grid_spec = pltpu.PrefetchScalarGridSpec(
        num_scalar_prefetch=4,
        grid=(n_tiles,),
        in_specs=[
            pl.BlockSpec(memory_space=pl.ANY),
            pl.BlockSpec((1, D_MODEL), lambda i, *_: (0, 0)),
            pl.BlockSpec((tile, 2), lambda i, *_: (i, 0)),
            w_in(0), w_in(0), w_dn(0), w_in(1), w_in(1), w_dn(1),
        ],
        out_specs=pl.BlockSpec(memory_space=pl.ANY),
        scratch_shapes=[
            pltpu.VMEM((2, tile, D_MODEL), F32),
            pltpu.VMEM((2, tile, D_MODEL), F32),
            pltpu.SemaphoreType.DMA((2,)),
            pltpu.SemaphoreType.DMA((2,)),
        ],
    )
    return pl.pallas_call(
        functools.partial(_moe_kernel, tile=tile, n_tiles=n_tiles),
        out_shape=jax.ShapeDtypeStruct((n, D_MODEL), F32),
        grid_spec=grid_spec,
        compiler_params=_cparams(("arbitrary",)),
        name="moe",
    )(tile_elo, tile_ehi, tile_nv, src, x2, norm_ffn, cw_sorted, wg, wu, wd, wg, wu, wd)


def _moe_plan(cls, n, tile):
    n_tiles = (n + N_CLASSES * (tile - 1)) // tile
    order = jnp.argsort(cls, stable=True).astype(jnp.int32)
    counts = jnp.sum((cls[:, None] == jnp.arange(N_CLASSES, dtype=jnp.int32)[None, :]).astype(jnp.int32), axis=0)
    tiles_per = (counts + tile - 1) // tile
    tile_end = jnp.cumsum(tiles_per)
    tile_start = tile_end - tiles_per
    tok_start = jnp.cumsum(counts) - counts
    t = jnp.arange(n_tiles, dtype=jnp.int32)
    tile_cls = jnp.minimum(jnp.sum((t[:, None] >= tile_end[None, :]).astype(jnp.int32), axis=1), N_CLASSES - 1)
    rank0 = (t - tile_start[tile_cls]) * tile
    tile_nv = jnp.clip(counts[tile_cls] - rank0, 0, tile).astype(jnp.int32)
    tile_nv = jnp.where(t < tile_end[-1], tile_nv, 0)
    r = jnp.arange(tile, dtype=jnp.int32)
    pos = tok_start[tile_cls][:, None] + rank0[:, None] + r[None, :]
    valid = r[None, :] < tile_nv[:, None]
    src = jnp.where(valid, order[jnp.clip(pos, 0, n - 1)], 0).reshape(-1).astype(jnp.int32)
    pair_lo = jnp.array([0, 0, 0, 1, 1, 2], jnp.int32)
    pair_hi = jnp.array([1, 2, 3, 2, 3, 3], jnp.int32)
    grp = tile_cls // N_PAIRS
    tile_elo = (grp * EXPERTS_PER_GROUP + pair_lo[tile_cls % N_PAIRS]).astype(jnp.int32)
    tile_ehi = (grp * EXPERTS_PER_GROUP + pair_hi[tile_cls % N_PAIRS]).astype(jnp.int32)
    return tile_elo, tile_ehi, tile_nv, src


def kernel(x, mem, positions, norm_mix, norm_mem, w_in, b_gate, qn_dil, kn_dil, qn_mem, kn_mem, w_mem_kv,
           w_o_dil, w_o_sb, w_o_mem, w_out, norm_ffn, w_router_group, b_router_group, w_router_expert,
           b_router_expert, w_exp_gate, w_exp_up, w_exp_down):
    b, s, d = x.shape
    n = b * s
    assert d == D_MODEL and w_in.shape == (1, D_MODEL, IN_COLS) and s % (BLOCK_Q * DILATIONS[-1]) == 0
    off_g = IN_COLS - GATE_COLS
    x2d = x.reshape(n, d)

    w_in_p = jnp.concatenate([w_in[0][:, off_g:], w_in[0][:, :off_g]], axis=1).astype(BF16)
    proj = _in_proj(x2d, norm_mix, w_in_p)
    proj3 = proj.reshape(b, s, IN_COLS)

    inv_freq = ROPE_THETA ** (-jnp.arange(0, HEAD_DIM, 2, dtype=F32) / HEAD_DIM)
    invf = jnp.tile(inv_freq, LANES // (HEAD_DIM // 2))[None, :]
    sign = jnp.tile(jnp.concatenate([-jnp.ones(HEAD_DIM // 2, F32), jnp.ones(HEAD_DIM // 2, F32)]), 2)[None, :]
    pos_b = jnp.broadcast_to(positions.astype(F32).reshape(n, 1), (n, LANES))
    cos_t, sin_t = _rope_tables(pos_b, invf, sign)
    o_dil = _dilated(proj3, cos_t.reshape(b, s, LANES), sin_t.reshape(b, s, LANES),
                     jnp.tile(qn_dil, (1, 2)), jnp.tile(kn_dil, (1, 2)))

    o_sb = _stick_breaking(proj3)

    k_mem, v_mem = _mem_kv(mem, norm_mem, w_mem_kv[0].astype(BF16), kn_mem)
    o_mem = _mem_attn(proj3, k_mem, v_mem, qn_mem)

    w_rt = jnp.zeros((32, D_MODEL), F32).at[:N_GROUPS].set(w_router_group[0].T)
    w_rt = w_rt.at[N_GROUPS:N_GROUPS + N_EXPERTS].set(w_router_expert[0].T)
    b_rt = jnp.zeros((32, 1), F32).at[:N_GROUPS, 0].set(b_router_group[0])
    b_rt = b_rt.at[N_GROUPS:N_GROUPS + N_EXPERTS, 0].set(b_router_expert[0])
    x2, route = _merge(x2d, proj, b_gate, o_dil.reshape(n, DIL_WIDTH), o_sb.reshape(n, SB_WIDTH),
                       o_mem.reshape(n, MEM_WIDTH), w_o_dil[0].astype(BF16), w_o_sb[0].astype(BF16),
                       w_o_mem[0].astype(BF16), w_out[0].astype(BF16), norm_ffn, w_rt, b_rt)

    cls = route[2].astype(jnp.int32)
    tile_elo, tile_ehi, tile_nv, src = _moe_plan(cls, n, MOE_TILE)
    cw_sorted = route[:2].T[src]
    out = _moe(x2, norm_ffn, cw_sorted, w_exp_gate[0].astype(BF16), w_exp_up[0].astype(BF16),
               w_exp_down[0].astype(BF16), tile_elo, tile_ehi, tile_nv, src)
    return out.reshape(b, s, d)
```

```python
import functools

import jax
import jax.numpy as jnp
from jax import lax
from jax.experimental import pallas as pl
from jax.experimental.pallas import tpu as pltpu

F32 = jnp.float32
BF16 = jnp.bfloat16

D_MODEL = 1024
HEAD_DIM = 64
LANES = 128
BLOCK_Q = 128
ROPE_THETA = 10000.0
EPS = 1e-6
DILATIONS = (1, 4, 16)
DIL_WIDTH = 512
SB_WIDTH = 512
MEM_WIDTH = 512
MEM_HEADS = 4
MEM_HEAD_DIM = 128
MEM_LEN = 256
GATE_COLS = 3 * D_MODEL
A_COLS = 9 * DIL_WIDTH
B_COLS = 3 * SB_WIDTH
IN_COLS = A_COLS + B_COLS + MEM_WIDTH + GATE_COLS
CB_DIL = GATE_COLS // LANES
CB_SB = CB_DIL + A_COLS // LANES
CB_MEM = CB_SB + B_COLS // LANES
N_GROUPS = 4
EXPERTS_PER_GROUP = 4
N_EXPERTS = 16
D_EXPERT = 512
N_PAIRS = 6
N_CLASSES = N_GROUPS * N_PAIRS
MOE_TILE = 256
X2_COLS = D_MODEL + LANES
NEG = -1e30
VMEM_LIMIT = 56 * 1024 * 1024


def _cparams(sem):
    return pltpu.CompilerParams(dimension_semantics=sem, vmem_limit_bytes=VMEM_LIMIT)


def _dot(a, b):
    return jnp.dot(a, b, preferred_element_type=F32)


def _dot_nt(a, b):
    return lax.dot_general(a, b, (((1,), (1,)), ((), ())), preferred_element_type=F32)


def _split_bf16(a):
    hi = a.astype(BF16)
    return hi, (a - hi.astype(F32)).astype(BF16)


def _head_masks(rows):
    lane = lax.broadcasted_iota(jnp.int32, (rows, LANES), 1)
    return lane < HEAD_DIM


def _inproj_kernel(x_ref, g_ref, w_ref, o_ref, *, chunk):
    x = x_ref[...]
    ms = jnp.mean(x * x, axis=-1, keepdims=True)
    h = (x * lax.rsqrt(ms + EPS) * g_ref[...]).astype(BF16)
    for c in range(IN_COLS // chunk):
        sl = slice(c * chunk, (c + 1) * chunk)
        o_ref[:, sl] = _dot(h, w_ref[:, sl]).astype(BF16)


def _in_proj(x2d, gain, w_bf16, tm=256, chunk=512):
    n = x2d.shape[0]
    return pl.pallas_call(
        functools.partial(_inproj_kernel, chunk=chunk),
        out_shape=jax.ShapeDtypeStruct((n, IN_COLS), BF16),
        grid=(n // tm,),
        in_specs=[
            pl.BlockSpec((tm, D_MODEL), lambda i: (i, 0)),
            pl.BlockSpec((1, D_MODEL), lambda i: (0, 0)),
            pl.BlockSpec((D_MODEL, IN_COLS), lambda i: (0, 0), pipeline_mode=pl.Buffered(1)),
        ],
        out_specs=pl.BlockSpec((tm, IN_COLS), lambda i: (i, 0)),
        compiler_params=_cparams(("parallel",)),
        name="in_proj",
    )(x2d, gain, w_bf16)


def _rope_kernel(pos_ref, invf_ref, sign_ref, cos_ref, sin_ref):
    ang = pos_ref[...] * invf_ref[...]
    cos_ref[...] = jnp.cos(ang)
    sin_ref[...] = jnp.sin(ang) * sign_ref[...]


def _rope_tables(pos_b, invf, sign, tm=2048):
    n = pos_b.shape[0]
    row = pl.BlockSpec((tm, LANES), lambda i: (i, 0))
    vec = pl.BlockSpec((1, LANES), lambda i: (0, 0))
    return pl.pallas_call(
        _rope_kernel,
        out_shape=(jax.ShapeDtypeStruct((n, LANES), F32),) * 2,
        grid=(n // tm,),
        in_specs=[row, vec, vec],
        out_specs=(row, row),
        compiler_params=_cparams(("parallel",)),
        name="rope_tab",
    )(pos_b, invf, sign)


def _dil_kernel(q0, k0, v0, q1, k1, v1, q2, k2, v2, cos_ref, sin_ref, qn_ref, kn_ref, o_ref,
                qp, km0, km1, vm0, vm1, st, stq, stk, stv, stg, *, seq):
    qkv = ((q0, k0, v0), (q1, k1, v1), (q2, k2, v2))
    chunk = 2 * BLOCK_Q
    n_chunks = seq // chunk

    gi = lax.broadcasted_iota(jnp.int32, (LANES, LANES), 0) // HEAD_DIM
    gj = lax.broadcasted_iota(jnp.int32, (LANES, LANES), 1) // HEAD_DIM
    head_mean = jnp.where(gi == gj, 1.0 / HEAD_DIM, 0.0).astype(BF16)
    lane_c = lax.broadcasted_iota(jnp.int32, (chunk, LANES), 1)
    first_half = (lane_c % HEAD_DIM) < (HEAD_DIM // 2)

    def norm_rope(t_ref, gain, rows, scale):
        t = t_ref[0, rows, :].astype(F32)
        hi, lo = _split_bf16(t * t)
        tn = t * lax.rsqrt(_dot(hi, head_mean) + _dot(lo, head_mean) + EPS) * gain
        rot = jnp.where(first_half, pltpu.roll(tn, LANES - HEAD_DIM // 2, 1), pltpu.roll(tn, HEAD_DIM // 2, 1))
        out = tn * cos_ref[0, rows, :] + rot * sin_ref[0, rows, :]
        return out * scale if scale != 1.0 else out

    def put(g, dst, qb, kb, vb):
        h = _head_masks(kb.shape[0])
        zero = jnp.zeros_like(kb)
        qp[g, dst, :] = qb
        km0[g, dst, :] = jnp.where(h, kb, zero)
        km1[g, dst, :] = jnp.where(h, zero, kb)
        vm0[g, dst, :] = jnp.where(h, vb, zero)
        vm1[g, dst, :] = jnp.where(h, zero, vb)

    def prep(i, _):
        rows = pl.ds(pl.multiple_of(i * chunk, chunk), chunk)
        for g, dil in enumerate(DILATIONS):
            q_ref, k_ref, v_ref = qkv[g]
            qn = norm_rope(q_ref, qn_ref[...], rows, HEAD_DIM ** -0.5)
            kn = norm_rope(k_ref, kn_ref[...], rows, 1.0)
            if dil == 1:
                put(g, rows, qn.astype(BF16), kn.astype(BF16), v_ref[0, rows, :])
                continue
            stq[g - 1] = qn
            stk[g - 1] = kn
            stv[g - 1] = v_ref[0, rows, :].astype(F32)
            piece, sub = chunk // dil, seq // dil
            for c in range(dil):
                src = pl.ds(c, piece, stride=dil)
                dst = pl.ds(pl.multiple_of(c * sub + i * piece, piece), piece)
                put(g, dst, stq[g - 1, src, :].astype(BF16), stk[g - 1, src, :].astype(BF16),
                    stv[g - 1, src, :].astype(BF16))
        return 0

    lax.fori_loop(0, n_chunks, prep, 0)

    qi = lax.broadcasted_iota(jnp.int32, (BLOCK_Q, BLOCK_Q), 0)
    kj = lax.broadcasted_iota(jnp.int32, (BLOCK_Q, BLOCK_Q), 1)
    bias_cur = jnp.where(kj <= qi, 0.0, NEG).astype(F32)
    bias_prev = jnp.where(kj >= qi, 0.0, NEG).astype(F32)
    bias = {BLOCK_Q: jnp.concatenate([bias_cur] * 2, axis=1),
            2 * BLOCK_Q: jnp.concatenate([bias_prev, bias_cur] * 2, axis=1)}
    hq = _head_masks(BLOCK_Q)
    ones = {nk: jnp.where(_head_masks(nk), 1.0, 0.0).astype(BF16) for nk in bias}

    def scores(g, q_lo, k_lo, nk):
        keys = jnp.concatenate([km0[g, k_lo:k_lo + nk, :], km1[g, k_lo:k_lo + nk, :]], axis=0)
        return _dot_nt(qp[g, q_lo:q_lo + BLOCK_Q, :], keys)

    def softmax(s, nk):
        s = s + bias[nk]
        mx = [jnp.max(s[:, h * nk:(h + 1) * nk], axis=-1, keepdims=True) for h in range(2)]
        m_all = jnp.concatenate([jnp.broadcast_to(m, (BLOCK_Q, nk)) for m in mx], axis=1)
        return jnp.exp(s - m_all).astype(BF16), jnp.where(hq, mx[0], mx[1])

    def weighted(p, mt, g, q_lo, k_lo, nk):
        rhs = jnp.concatenate([
            jnp.concatenate([vm0[g, k_lo:k_lo + nk, :], ones[nk]], axis=1),
            jnp.concatenate([vm1[g, k_lo:k_lo + nk, :], 1.0 - ones[nk]], axis=1)], axis=0)
        pv = _dot(p, rhs)
        st[g, 0, q_lo:q_lo + BLOCK_Q, :] = pv[:, :LANES]
        st[g, 1, q_lo:q_lo + BLOCK_Q, :] = pv[:, LANES:]
        st[g, 2, q_lo:q_lo + BLOCK_Q, :] = mt

    blocks = []
    for g, dil in enumerate(DILATIONS):
        sub = seq // dil
        for c in range(dil):
            for n in range(sub // BLOCK_Q):
                q_lo = c * sub + n * BLOCK_Q
                blocks.append((g, q_lo, q_lo - BLOCK_Q, 2 * BLOCK_Q) if n else (g, q_lo, q_lo, BLOCK_Q))
    ss, ps = {}, {}
    for i in range(len(blocks) + 2):
        if i >= 2:
            weighted(*ps.pop(i - 2), *blocks[i - 2])
        if 1 <= i <= len(blocks):
            ps[i - 1] = softmax(ss.pop(i - 1), blocks[i - 1][3])
        if i < len(blocks):
            ss[i] = scores(*blocks[i])

    def merge(i, _):
        rows = pl.ds(pl.multiple_of(i * chunk, chunk), chunk)
        parts = []
        for g, dil in enumerate(DILATIONS):
            if dil == 1:
                parts.append([st[g, k, rows, :] for k in range(3)])
                continue
            piece, sub = chunk // dil, seq // dil
            for c in range(dil):
                src = pl.ds(pl.multiple_of(c * sub + i * piece, piece), piece)
                dst = pl.ds(c, piece, stride=dil)
                for k in range(3):
                    stg[g - 1, k, dst, :] = st[g, k, src, :]
            parts.append([stg[g - 1, k] for k in range(3)])
        m_max = functools.reduce(jnp.maximum, [m for _, _, m in parts])
        num = den = None
        for pv, rowsum, m in parts:
            w = jnp.exp(m - m_max)
            num = w * pv if num is None else num + w * pv
            den = w * rowsum if den is None else den + w * rowsum
        o_ref[0, rows, :] = (num / den).astype(BF16)
        return 0

    lax.fori_loop(0, n_chunks, merge, 0)


def _dilated(proj3, cos3, sin3, qn2, kn2):
    b, s, _ = proj3.shape
    specs = []
    for g in range(3):
        for t in range(3):
            cb = CB_DIL + (t * 3 + g) * 4
            specs.append(pl.BlockSpec((1, s, LANES), lambda bi, j, cb=cb: (bi, 0, cb + j)))
    tab = pl.BlockSpec((1, s, LANES), lambda bi, j: (bi, 0, 0))
    vec = pl.BlockSpec((1, LANES), lambda bi, j: (0, 0))
    chunk = 2 * BLOCK_Q
    return pl.pallas_call(
        functools.partial(_dil_kernel, seq=s),
        out_shape=jax.ShapeDtypeStruct((b, s, DIL_WIDTH), BF16),
        grid=(b, DIL_WIDTH // LANES),
        in_specs=specs + [tab, tab, vec, vec],
        out_specs=pl.BlockSpec((1, s, LANES), lambda bi, j: (bi, 0, j)),
        scratch_shapes=[pltpu.VMEM((3, s, LANES), BF16)] * 5
        + [pltpu.VMEM((3, 3, s, LANES), F32)]
        + [pltpu.VMEM((2, chunk, LANES), F32)] * 3
        + [pltpu.VMEM((2, 3, chunk, LANES), F32)],
        compiler_params=_cparams(("parallel", "arbitrary")),
        name="dilated",
    )(*([proj3] * 9), cos3, sin3, qn2, kn2)


def _sb_kernel(q_ref, k_ref, v_ref, o_ref, qs, ks, vs, *, seq):
    nb = seq // BLOCK_Q
    qi = lax.broadcasted_iota(jnp.int32, (BLOCK_Q, 2 * BLOCK_Q), 0)
    kj = lax.broadcasted_iota(jnp.int32, (BLOCK_Q, 2 * BLOCK_Q), 1) % BLOCK_Q
    causal = kj < qi
    uj = lax.broadcasted_iota(jnp.int32, (2 * BLOCK_Q, 2 * BLOCK_Q), 0) % BLOCK_Q
    us = lax.broadcasted_iota(jnp.int32, (2 * BLOCK_Q, 2 * BLOCK_Q), 1)
    suffix = jnp.where((uj > us) | (us >= BLOCK_Q), 1.0, 0.0).astype(BF16)
    h0 = _head_masks(BLOCK_Q)

    def prep(i, _):
        rows = pl.ds(pl.multiple_of(i * BLOCK_Q, BLOCK_Q), BLOCK_Q)
        qs[rows, :] = q_ref[0, rows, :] * jnp.asarray(HEAD_DIM ** -0.5, BF16)
        k = k_ref[0, rows, :]
        v = v_ref[0, rows, :]
        zero = jnp.zeros_like(k)
        ks[i, :BLOCK_Q, :] = jnp.where(h0, k, zero)
        ks[i, BLOCK_Q:, :] = jnp.where(h0, zero, k)
        vs[i, :BLOCK_Q, :] = jnp.where(h0, v, zero)
        vs[i, BLOCK_Q:, :] = jnp.where(h0, zero, v)
        return 0

    lax.fori_loop(0, nb, prep, 0)

    def scores(rb, kb):
        return _dot_nt(qs[pl.ds(rb * BLOCK_Q, BLOCK_Q), :], ks[kb])

    def log_weights(z, diag):
        neg = jnp.minimum(z, 0.0)
        sp = jnp.log(1.0 + jnp.exp(-jnp.abs(z)))
        log_beta = neg - sp
        log_1m = (neg - z) - sp
        if diag:
            log_1m = jnp.where(causal, log_1m, 0.0)
        hi, lo = _split_bf16(log_1m)
        lhs = jnp.concatenate([jnp.concatenate([hi[:, sl], lo[:, sl]], axis=1)
                               for sl in (slice(0, BLOCK_Q), slice(BLOCK_Q, 2 * BLOCK_Q))], axis=0)
        su = _dot(lhs, suffix)
        w = log_beta + jnp.concatenate([su[:BLOCK_Q, :BLOCK_Q], su[BLOCK_Q:, :BLOCK_Q]], axis=1)
        tot = jnp.concatenate([su[:BLOCK_Q, BLOCK_Q:], su[BLOCK_Q:, BLOCK_Q:]], axis=1)
        return w, tot

    def accumulate(state, w, tot, rb, kb):
        diag = kb == rb
        c, acc = (None, None) if diag else state
        a = jnp.exp(w if diag else w + c)
        if diag:
            a = jnp.where(causal, a, 0.0)
        pv = _dot(a.astype(BF16), vs[kb])
        acc = pv if diag else acc + pv
        if kb == 0:
            o_ref[0, pl.ds(rb * BLOCK_Q, BLOCK_Q), :] = acc.astype(BF16)
        elif diag:
            c = tot
        else:
            c = c + tot
        return c, acc

    pairs = [(rb, kb) for rb in range(nb) for kb in range(rb, -1, -1)]
    zs, wts, state = {}, {}, None
    for i in range(len(pairs) + 2):
        if i >= 2:
            state = accumulate(state, *wts.pop(i - 2), *pairs[i - 2])
        if 1 <= i <= len(pairs):
            rb, kb = pairs[i - 1]
            wts[i - 1] = log_weights(zs.pop(i - 1), kb == rb)
        if i < len(pairs):
            zs[i] = scores(*pairs[i])


def _stick_breaking(proj3):
    b, s, _ = proj3.shape
    nb = s // BLOCK_Q
    specs = [pl.BlockSpec((1, s, LANES), lambda bi, j, cb=CB_SB + t * 4: (bi, 0, cb + j)) for t in range(3)]
    return pl.pallas_call(
        functools.partial(_sb_kernel, seq=s),
        out_shape=jax.ShapeDtypeStruct((b, s, SB_WIDTH), BF16),
        grid=(b, SB_WIDTH // LANES),
        in_specs=specs,
        out_specs=pl.BlockSpec((1, s, LANES), lambda bi, j: (bi, 0, j)),
        scratch_shapes=[pltpu.VMEM((s, LANES), BF16)] + [pltpu.VMEM((nb, 2 * BLOCK_Q, LANES), BF16)] * 2,
        compiler_params=_cparams(("parallel", "arbitrary")),
        name="stickbrk",
    )(proj3, proj3, proj3)


def _memkv_kernel(mem_ref, g_ref, w_ref, kn_ref, k_ref, v_ref):
    x = mem_ref[0]
    ms = jnp.mean(x * x, axis=-1, keepdims=True)
    h = (x * lax.rsqrt(ms + EPS) * g_ref[...]).astype(BF16)
    kv = _dot(h, w_ref[...])
    for hd in range(MEM_HEADS):
        sl = slice(hd * MEM_HEAD_DIM, (hd + 1) * MEM_HEAD_DIM)
        kh = kv[:, sl]
        msk = jnp.mean(kh * kh, axis=-1, keepdims=True)
        k_ref[0, :, sl] = (kh * lax.rsqrt(msk + EPS) * kn_ref[...]).astype(BF16)
    v_ref[0] = kv[:, MEM_WIDTH:].astype(BF16)


def _mem_kv(mem, gain, w_bf16, kn):
    b = mem.shape[0]
    out = jax.ShapeDtypeStruct((b, MEM_LEN, MEM_WIDTH), BF16)
    blk = pl.BlockSpec((1, MEM_LEN, MEM_WIDTH), lambda i: (i, 0, 0))
    return pl.pallas_call(
        _memkv_kernel,
        out_shape=(out, out),
        grid=(b,),
        in_specs=[
            pl.BlockSpec((1, MEM_LEN, D_MODEL), lambda i: (i, 0, 0)),
            pl.BlockSpec((1, D_MODEL), lambda i: (0, 0)),
            pl.BlockSpec((D_MODEL, 2 * MEM_WIDTH), lambda i: (0, 0)),
            pl.BlockSpec((1, MEM_HEAD_DIM), lambda i: (0, 0)),
        ],
        out_specs=(blk, blk),
        compiler_params=_cparams(("parallel",)),
        name="mem_kv",
    )(mem, gain, w_bf16, kn)


def _memattn_kernel(q_ref, k_ref, v_ref, qn_ref, o_ref):
    q = q_ref[0].astype(F32)
    ms = jnp.mean(q * q, axis=-1, keepdims=True)
    qn = (q * lax.rsqrt(ms + EPS) * qn_ref[...]).astype(BF16)
    s = _dot_nt(qn, k_ref[0]) * (MEM_HEAD_DIM ** -0.5)
    m = jnp.max(s, axis=-1, keepdims=True)
    p = jnp.exp(s - m)
    den = jnp.sum(p, axis=-1, keepdims=True)
    o_ref[0] = (_dot(p.astype(BF16), v_ref[0]) / den).astype(BF16)


def _mem_attn(proj3, k, v, qn, tq=1024):
    b, s, _ = proj3.shape
    kv = pl.BlockSpec((1, MEM_LEN, MEM_HEAD_DIM), lambda bi, h, i: (bi, 0, h))
    return pl.pallas_call(
        _memattn_kernel,
        out_shape=jax.ShapeDtypeStruct((b, s, MEM_WIDTH), BF16),
        grid=(b, MEM_HEADS, s // tq),
        in_specs=[
            pl.BlockSpec((1, tq, MEM_HEAD_DIM), lambda bi, h, i: (bi, i, CB_MEM + h)),
            kv, kv,
            pl.BlockSpec((1, MEM_HEAD_DIM), lambda bi, h, i: (0, 0)),
        ],
        out_specs=pl.BlockSpec((1, tq, MEM_HEAD_DIM), lambda bi, h, i: (bi, i, h)),
        compiler_params=_cparams(("parallel", "parallel", "arbitrary")),
        name="mem_attn",
    )(proj3, k, v, qn)


def _merge_kernel(x_ref, gl_ref, bg_ref, od_ref, os_ref, om_ref, wd_ref, ws_ref, wm_ref, wo_ref,
                  nf_ref, wr_ref, br_ref, x2_ref, r_ref):
    merged = None
    for i, (o_ref, w_ref) in enumerate(((od_ref, wd_ref), (os_ref, ws_ref), (om_ref, wm_ref))):
        sl = slice(i * D_MODEL, (i + 1) * D_MODEL)
        gate = jax.nn.sigmoid(gl_ref[:, sl].astype(F32) + bg_ref[:, sl])
        term = gate * _dot(o_ref[...], w_ref[...])
        merged = term if merged is None else merged + term
    x2 = x_ref[...] + _dot(merged.astype(BF16), wo_ref[...])
    x2_ref[:, :D_MODEL] = x2

    ms = jnp.mean(x2 * x2, axis=-1, keepdims=True)
    hn = x2 * lax.rsqrt(ms + EPS) * nf_ref[...]
    lt = lax.dot_general(wr_ref[...], hn, (((1,), (1,)), ((), ())), precision=lax.Precision.HIGHEST,
                         preferred_element_type=F32) + br_ref[...]
    row = lambda i: lt[i:i + 1, :]
    first_max = lambda v, mx: jnp.where(v[0] == mx, 0, jnp.where(v[1] == mx, 1, jnp.where(v[2] == mx, 2, 3)))
    gl = [row(i) for i in range(N_GROUPS)]
    gmax = functools.reduce(jnp.maximum, gl)
    g_top = 1.0 / functools.reduce(lambda a, c: a + c, [jnp.exp(v - gmax) for v in gl])
    g_idx = first_max(gl, gmax)
    el = []
    for e in range(EXPERTS_PER_GROUP):
        v = row(N_GROUPS + 3 * EXPERTS_PER_GROUP + e)
        for g in range(N_GROUPS - 2, -1, -1):
            v = jnp.where(g_idx == g, row(N_GROUPS + g * EXPERTS_PER_GROUP + e), v)
        el.append(v)
    emax = functools.reduce(jnp.maximum, el)
    a_idx = first_max(el, emax)
    rest = [jnp.where(a_idx == e, -jnp.inf, el[e]) for e in range(EXPERTS_PER_GROUP)]
    rmax = functools.reduce(jnp.maximum, rest)
    b_idx = first_max(rest, rmax)
    ratio = jnp.exp(rmax - emax)
    w_a = g_top / (1.0 + ratio)
    w_b = g_top * ratio / (1.0 + ratio)
    lo = jnp.minimum(a_idx, b_idx)
    hi = jnp.maximum(a_idx, b_idx)
    pair = jnp.where(lo == 0, hi - 1, jnp.where(lo == 1, hi + 1, 5))
    cls = g_idx * N_PAIRS + pair
    a_first = a_idx < b_idx
    tm = lt.shape[1]
    ri = lax.broadcasted_iota(jnp.int32, (8, tm), 0)
    info = jnp.where(ri == 0, jnp.where(a_first, w_a, w_b),
                     jnp.where(ri == 1, jnp.where(a_first, w_b, w_a), jnp.where(ri == 2, cls.astype(F32), 0.0)))
    r_ref[...] = info
    x2_ref[:, D_MODEL:] = jnp.concatenate([info, jnp.zeros((LANES - 8, tm), F32)], axis=0).T


def _merge(x2d, proj, b_gate, o_dil, o_sb, o_mem, w_o_dil, w_o_sb, w_o_mem, w_out, norm_ffn, w_rt, b_rt, tm=256):
    n = x2d.shape[0]
    row = lambda w: pl.BlockSpec((tm, w), lambda i: (i, 0))
    full = lambda a, bdim: pl.BlockSpec((a, bdim), lambda i: (0, 0))
    return pl.pallas_call(
        _merge_kernel,
        out_shape=(jax.ShapeDtypeStruct((n, X2_COLS), F32), jax.ShapeDtypeStruct((8, n), F32)),
        grid=(n // tm,),
        in_specs=[
            row(D_MODEL), row(GATE_COLS), full(1, GATE_COLS),
            row(DIL_WIDTH), row(SB_WIDTH), row(MEM_WIDTH),
            full(DIL_WIDTH, D_MODEL), full(SB_WIDTH, D_MODEL), full(MEM_WIDTH, D_MODEL), full(D_MODEL, D_MODEL),
            full(1, D_MODEL), full(32, D_MODEL), full(32, 1),
        ],
        out_specs=(row(X2_COLS), pl.BlockSpec((8, tm), lambda i: (0, i))),
        compiler_params=_cparams(("parallel",)),
        name="merge",
    )(x2d, proj, b_gate, o_dil, o_sb, o_mem, w_o_dil, w_o_sb, w_o_mem, w_out, norm_ffn, w_rt, b_rt)


def _moe_kernel(elo_ref, ehi_ref, nv_ref, src_ref,
                x_hbm, nf_ref, wgl_ref, wul_ref, wdl_ref, wgh_ref, wuh_ref, wdh_ref,
                out_hbm, xbuf, obuf, gsem, ssem, *, tile, n_tiles):
    i = pl.program_id(0)
    slot = i % 2

    def row_in(t, r, s):
        return pltpu.make_async_copy(x_hbm.at[pl.ds(src_ref[t * tile + r], 1), :],
                                     xbuf.at[s, pl.ds(r, 1), :], gsem.at[s])

    def row_out(t, r, s):
        return pltpu.make_async_copy(obuf.at[s, pl.ds(r, 1), :],
                                     out_hbm.at[pl.ds(src_ref[t * tile + r], 1), :], ssem.at[s])

    def gather(t, s):
        @pl.when(nv_ref[t] > 0)
        def _():
            def body(r, _):
                row_in(t, r, s).start()
                return 0
            lax.fori_loop(0, tile, body, 0)

    def drain_out(t, s):
        def body(r, _):
            row_out(t, r, s).wait()
            return 0
        lax.fori_loop(0, nv_ref[t], body, 0)

    @pl.when(i == 0)
    def _():
        gather(0, 0)

    @pl.when(i + 1 < n_tiles)
    def _():
        gather(i + 1, 1 - slot)

    @pl.when(i >= 2)
    def _():
        drain_out(i - 2, slot)

    nv = nv_ref[i]

    @pl.when(nv > 0)
    def _():
        def body(r, _):
            row_in(i, r, slot).wait()
            return 0
        lax.fori_loop(0, tile, body, 0)

        x = xbuf[slot, :, :D_MODEL]
        cw = xbuf[slot, :, D_MODEL:]
        ms = jnp.mean(x * x, axis=-1, keepdims=True)
        hn = (x * lax.rsqrt(ms + EPS) * nf_ref[...]).astype(BF16)
        moe = None
        for c, (wg, wu, wd) in enumerate(((wgl_ref, wul_ref, wdl_ref), (wgh_ref, wuh_ref, wdh_ref))):
            gate = _dot(hn, wg[0])
            up = _dot(hn, wu[0])
            act = (gate * jax.nn.sigmoid(gate) * up).astype(BF16)
            term = cw[:, c:c + 1] * _dot(act, wd[0])
            moe = term if moe is None else moe + term
        obuf[slot] = x + moe

        def sbody(r, _):
            row_out(i, r, slot).start()
            return 0
        lax.fori_loop(0, nv, sbody, 0)

    @pl.when(i == n_tiles - 1)
    def _():
        @pl.when(i >= 1)
        def _():
            drain_out(i - 1, 1 - slot)
        drain_out(i, slot)


def _moe(x2e, norm_ffn, wg, wu, wd, tile_elo, tile_ehi, tile_nv, src, tile=MOE_TILE):
    n = x2e.shape[0]
    n_tiles = tile_nv.shape[0]
    w_in = lambda sel: pl.BlockSpec((1, D_MODEL, D_EXPERT), lambda i, elo, ehi, nv, s: ((elo, ehi)[sel][i], 0, 0))
    w_dn = lambda sel: pl.BlockSpec((1, D_EXPERT, D_MODEL), lambda i, elo, ehi, nv, s: ((elo, ehi)[sel][i], 0, 0))
    grid_spec = pltpu.PrefetchScalarGridSpec(
        num_scalar_prefetch=4,
        grid=(n_tiles,),
        in_specs=[
            pl.BlockSpec(memory_space=pl.ANY),
            pl.BlockSpec((1, D_MODEL), lambda i, *_: (0, 0)),
            w_in(0), w_in(0), w_dn(0), w_in(1), w_in(1), w_dn(1),
        ],
        out_specs=pl.BlockSpec(memory_space=pl.ANY),
        scratch_shapes=[
            pltpu.VMEM((2, tile, X2_COLS), F32),
            pltpu.VMEM((2, tile, D_MODEL), F32),
            pltpu.SemaphoreType.DMA((2,)),
            pltpu.SemaphoreType.DMA((2,)),
        ],
    )
    return pl.pallas_call(
        functools.partial(_moe_kernel, tile=tile, n_tiles=n_tiles),
        out_shape=jax.ShapeDtypeStruct((n, D_MODEL), F32),
        grid_spec=grid_spec,
        compiler_params=_cparams(("arbitrary",)),
        name="moe",
    )(tile_elo, tile_ehi, tile_nv, src, x2e, norm_ffn, wg, wu, wd, wg, wu, wd)


def _moe_plan(cls, n, tile):
    n_tiles = (n + N_CLASSES * (tile - 1)) // tile
    order = jnp.argsort(cls, stable=True).astype(jnp.int32)
    counts = jnp.sum((cls[:, None] == jnp.arange(N_CLASSES, dtype=jnp.int32)[None, :]).astype(jnp.int32), axis=0)
    tiles_per = (counts + tile - 1) // tile
    tile_end = jnp.cumsum(tiles_per)
    tile_start = tile_end - tiles_per
    tok_start = jnp.cumsum(counts) - counts
    t = jnp.arange(n_tiles, dtype=jnp.int32)
    tile_cls = jnp.minimum(jnp.sum((t[:, None] >= tile_end[None, :]).astype(jnp.int32), axis=1), N_CLASSES - 1)
    rank0 = (t - tile_start[tile_cls]) * tile
    tile_nv = jnp.clip(counts[tile_cls] - rank0, 0, tile).astype(jnp.int32)
    tile_nv = jnp.where(t < tile_end[-1], tile_nv, 0)
    r = jnp.arange(tile, dtype=jnp.int32)
    pos = tok_start[tile_cls][:, None] + rank0[:, None] + r[None, :]
    valid = r[None, :] < tile_nv[:, None]
    src = jnp.where(valid, order[jnp.clip(pos, 0, n - 1)], 0).reshape(-1).astype(jnp.int32)
    pair_lo = jnp.array([0, 0, 0, 1, 1, 2], jnp.int32)
    pair_hi = jnp.array([1, 2, 3, 2, 3, 3], jnp.int32)
    grp = tile_cls // N_PAIRS
    tile_elo = (grp * EXPERTS_PER_GROUP + pair_lo[tile_cls % N_PAIRS]).astype(jnp.int32)
    tile_ehi = (grp * EXPERTS_PER_GROUP + pair_hi[tile_cls % N_PAIRS]).astype(jnp.int32)
    return tile_elo, tile_ehi, tile_nv, src


def kernel(x, mem, positions, norm_mix, norm_mem, w_in, b_gate, qn_dil, kn_dil, qn_mem, kn_mem, w_mem_kv,
           w_o_dil, w_o_sb, w_o_mem, w_out, norm_ffn, w_router_group, b_router_group, w_router_expert,
           b_router_expert, w_exp_gate, w_exp_up, w_exp_down):
    b, s, d = x.shape
    n = b * s
    assert d == D_MODEL and w_in.shape == (1, D_MODEL, IN_COLS) and s % (BLOCK_Q * DILATIONS[-1]) == 0
    off_g = IN_COLS - GATE_COLS
    x2d = x.reshape(n, d)

    w_in_p = jnp.concatenate([w_in[0][:, off_g:], w_in[0][:, :off_g]], axis=1).astype(BF16)
    proj = _in_proj(x2d, norm_mix, w_in_p)
    proj3 = proj.reshape(b, s, IN_COLS)

    inv_freq = ROPE_THETA ** (-jnp.arange(0, HEAD_DIM, 2, dtype=F32) / HEAD_DIM)
    invf = jnp.tile(inv_freq, LANES // (HEAD_DIM // 2))[None, :]
    sign = jnp.tile(jnp.concatenate([-jnp.ones(HEAD_DIM // 2, F32), jnp.ones(HEAD_DIM // 2, F32)]), 2)[None, :]
    pos_b = jnp.broadcast_to(positions.astype(F32).reshape(n, 1), (n, LANES))
    cos_t, sin_t = _rope_tables(pos_b, invf, sign)
    o_dil = _dilated(proj3, cos_t.reshape(b, s, LANES), sin_t.reshape(b, s, LANES),
                     jnp.tile(qn_dil, (1, 2)), jnp.tile(kn_dil, (1, 2)))

    o_sb = _stick_breaking(proj3)

    k_mem, v_mem = _mem_kv(mem, norm_mem, w_mem_kv[0].astype(BF16), kn_mem)
    o_mem = _mem_attn(proj3, k_mem, v_mem, qn_mem)

    w_rt = jnp.zeros((32, D_MODEL), F32).at[:N_GROUPS].set(w_router_group[0].T)
    w_rt = w_rt.at[N_GROUPS:N_GROUPS + N_EXPERTS].set(w_router_expert[0].T)
    b_rt = jnp.zeros((32, 1), F32).at[:N_GROUPS, 0].set(b_router_group[0])
    b_rt = b_rt.at[N_GROUPS:N_GROUPS + N_EXPERTS, 0].set(b_router_expert[0])
    x2e, route = _merge(x2d, proj, b_gate, o_dil.reshape(n, DIL_WIDTH), o_sb.reshape(n, SB_WIDTH),
                        o_mem.reshape(n, MEM_WIDTH), w_o_dil[0].astype(BF16), w_o_sb[0].astype(BF16),
                        w_o_mem[0].astype(BF16), w_out[0].astype(BF16), norm_ffn, w_rt, b_rt)

    tile_elo, tile_ehi, tile_nv, src = _moe_plan(route[2].astype(jnp.int32), n, MOE_TILE)
    out = _moe(x2e, norm_ffn, w_exp_gate[0].astype(BF16), w_exp_up[0].astype(BF16),
               w_exp_down[0].astype(BF16), tile_elo, tile_ehi, tile_nv, src)
    return out.reshape(b, s, d)
```

```python
import functools

import jax
import jax.numpy as jnp
from jax import lax
from jax.experimental import pallas as pl
from jax.experimental.pallas import tpu as pltpu

F32 = jnp.float32
BF16 = jnp.bfloat16

D_MODEL = 1024
HEAD_DIM = 64
LANES = 128
BLOCK_Q = 128
ROPE_THETA = 10000.0
EPS = 1e-6
DILATIONS = (1, 4, 16)
DIL_WIDTH = 512
SB_WIDTH = 512
MEM_WIDTH = 512
MEM_HEADS = 4
MEM_HEAD_DIM = 128
MEM_LEN = 256
GATE_COLS = 3 * D_MODEL
A_COLS = 9 * DIL_WIDTH
B_COLS = 3 * SB_WIDTH
IN_COLS = A_COLS + B_COLS + MEM_WIDTH + GATE_COLS
CB_DIL = GATE_COLS // LANES
CB_SB = CB_DIL + A_COLS // LANES
CB_MEM = CB_SB + B_COLS // LANES
N_GROUPS = 4
EXPERTS_PER_GROUP = 4
N_EXPERTS = 16
D_EXPERT = 512
N_PAIRS = 6
N_CLASSES = N_GROUPS * N_PAIRS
MOE_TILE = 256
DMA_UNROLL = 8
X2_COLS = D_MODEL + LANES
NEG = -1e30
VMEM_LIMIT = 56 * 1024 * 1024


def _cparams(sem):
    return pltpu.CompilerParams(dimension_semantics=sem, vmem_limit_bytes=VMEM_LIMIT)


def _dot(a, b):
    return jnp.dot(a, b, preferred_element_type=F32)


def _dot_nt(a, b):
    return lax.dot_general(a, b, (((1,), (1,)), ((), ())), preferred_element_type=F32)


def _split_bf16(a):
    hi = a.astype(BF16)
    return hi, (a - hi.astype(F32)).astype(BF16)


def _head_masks(rows):
    lane = lax.broadcasted_iota(jnp.int32, (rows, LANES), 1)
    return lane < HEAD_DIM


def _inproj_kernel(x_ref, g_ref, w_ref, o_ref, *, chunk):
    x = x_ref[...]
    ms = jnp.mean(x * x, axis=-1, keepdims=True)
    h = (x * lax.rsqrt(ms + EPS) * g_ref[...]).astype(BF16)
    for c in range(IN_COLS // chunk):
        sl = slice(c * chunk, (c + 1) * chunk)
        o_ref[:, sl] = _dot(h, w_ref[:, sl]).astype(BF16)


def _in_proj(x2d, gain, w_bf16, tm=256, chunk=512):
    n = x2d.shape[0]
    return pl.pallas_call(
        functools.partial(_inproj_kernel, chunk=chunk),
        out_shape=jax.ShapeDtypeStruct((n, IN_COLS), BF16),
        grid=(n // tm,),
        in_specs=[
            pl.BlockSpec((tm, D_MODEL), lambda i: (i, 0)),
            pl.BlockSpec((1, D_MODEL), lambda i: (0, 0)),
            pl.BlockSpec((D_MODEL, IN_COLS), lambda i: (0, 0), pipeline_mode=pl.Buffered(1)),
        ],
        out_specs=pl.BlockSpec((tm, IN_COLS), lambda i: (i, 0)),
        compiler_params=_cparams(("parallel",)),
        name="in_proj",
    )(x2d, gain, w_bf16)


def _rope_kernel(pos_ref, invf_ref, sign_ref, cos_ref, sin_ref):
    ang = pos_ref[...] * invf_ref[...]
    cos_ref[...] = jnp.cos(ang)
    sin_ref[...] = jnp.sin(ang) * sign_ref[...]


def _rope_tables(pos_b, invf, sign, tm=2048):
    n = pos_b.shape[0]
    row = pl.BlockSpec((tm, LANES), lambda i: (i, 0))
    vec = pl.BlockSpec((1, LANES), lambda i: (0, 0))
    return pl.pallas_call(
        _rope_kernel,
        out_shape=(jax.ShapeDtypeStruct((n, LANES), F32),) * 2,
        grid=(n // tm,),
        in_specs=[row, vec, vec],
        out_specs=(row, row),
        compiler_params=_cparams(("parallel",)),
        name="rope_tab",
    )(pos_b, invf, sign)


def _dil_kernel(q0, k0, v0, q1, k1, v1, q2, k2, v2, cos_ref, sin_ref, qn_ref, kn_ref, o_ref,
                qp, km0, km1, vm0, vm1, st, stq, stk, stv, stg, *, seq):
    qkv = ((q0, k0, v0), (q1, k1, v1), (q2, k2, v2))
    chunk = 2 * BLOCK_Q
    n_chunks = seq // chunk

    gi = lax.broadcasted_iota(jnp.int32, (LANES, LANES), 0) // HEAD_DIM
    gj = lax.broadcasted_iota(jnp.int32, (LANES, LANES), 1) // HEAD_DIM
    head_mean = jnp.where(gi == gj, 1.0 / HEAD_DIM, 0.0).astype(BF16)
    lane_c = lax.broadcasted_iota(jnp.int32, (chunk, LANES), 1)
    first_half = (lane_c % HEAD_DIM) < (HEAD_DIM // 2)

    def norm_rope(t_ref, gain, rows, scale):
        t = t_ref[0, rows, :].astype(F32)
        hi, lo = _split_bf16(t * t)
        tn = t * lax.rsqrt(_dot(hi, head_mean) + _dot(lo, head_mean) + EPS) * gain
        rot = jnp.where(first_half, pltpu.roll(tn, LANES - HEAD_DIM // 2, 1), pltpu.roll(tn, HEAD_DIM // 2, 1))
        out = tn * cos_ref[0, rows, :] + rot * sin_ref[0, rows, :]
        return out * scale if scale != 1.0 else out

    def put(g, dst, qb, kb, vb):
        h = _head_masks(kb.shape[0])
        zero = jnp.zeros_like(kb)
        qp[g, dst, :] = qb
        km0[g, dst, :] = jnp.where(h, kb, zero)
        km1[g, dst, :] = jnp.where(h, zero, kb)
        vm0[g, dst, :] = jnp.where(h, vb, zero)
        vm1[g, dst, :] = jnp.where(h, zero, vb)

    def prep(i, _):
        rows = pl.ds(pl.multiple_of(i * chunk, chunk), chunk)
        for g, dil in enumerate(DILATIONS):
            q_ref, k_ref, v_ref = qkv[g]
            qn = norm_rope(q_ref, qn_ref[...], rows, HEAD_DIM ** -0.5)
            kn = norm_rope(k_ref, kn_ref[...], rows, 1.0)
            if dil == 1:
                put(g, rows, qn.astype(BF16), kn.astype(BF16), v_ref[0, rows, :])
                continue
            stq[g - 1] = qn
            stk[g - 1] = kn
            stv[g - 1] = v_ref[0, rows, :].astype(F32)
            piece, sub = chunk // dil, seq // dil
            for c in range(dil):
                src = pl.ds(c, piece, stride=dil)
                dst = pl.ds(pl.multiple_of(c * sub + i * piece, piece), piece)
                put(g, dst, stq[g - 1, src, :].astype(BF16), stk[g - 1, src, :].astype(BF16),
                    stv[g - 1, src, :].astype(BF16))
        return 0

    lax.fori_loop(0, n_chunks, prep, 0)

    qi = lax.broadcasted_iota(jnp.int32, (BLOCK_Q, BLOCK_Q), 0)
    kj = lax.broadcasted_iota(jnp.int32, (BLOCK_Q, BLOCK_Q), 1)
    bias_cur = jnp.where(kj <= qi, 0.0, NEG).astype(F32)
    bias_prev = jnp.where(kj >= qi, 0.0, NEG).astype(F32)
    bias = {BLOCK_Q: jnp.concatenate([bias_cur] * 2, axis=1),
            2 * BLOCK_Q: jnp.concatenate([bias_prev, bias_cur] * 2, axis=1)}
    hq = _head_masks(BLOCK_Q)
    ones = {nk: jnp.where(_head_masks(nk), 1.0, 0.0).astype(BF16) for nk in bias}

    def scores(g, q_lo, k_lo, nk):
        keys = jnp.concatenate([km0[g, k_lo:k_lo + nk, :], km1[g, k_lo:k_lo + nk, :]], axis=0)
        return _dot_nt(qp[g, q_lo:q_lo + BLOCK_Q, :], keys)

    def softmax(s, nk):
        s = s + bias[nk]
        mx = [jnp.max(s[:, h * nk:(h + 1) * nk], axis=-1, keepdims=True) for h in range(2)]
        m_all = jnp.concatenate([jnp.broadcast_to(m, (BLOCK_Q, nk)) for m in mx], axis=1)
        return jnp.exp(s - m_all).astype(BF16), jnp.where(hq, mx[0], mx[1])

    def weighted(p, mt, g, q_lo, k_lo, nk):
        rhs = jnp.concatenate([
            jnp.concatenate([vm0[g, k_lo:k_lo + nk, :], ones[nk]], axis=1),
            jnp.concatenate([vm1[g, k_lo:k_lo + nk, :], 1.0 - ones[nk]], axis=1)], axis=0)
        pv = _dot(p, rhs)
        st[g, 0, q_lo:q_lo + BLOCK_Q, :] = pv[:, :LANES]
        st[g, 1, q_lo:q_lo + BLOCK_Q, :] = pv[:, LANES:]
        st[g, 2, q_lo:q_lo + BLOCK_Q, :] = mt

    blocks = []
    for g, dil in enumerate(DILATIONS):
        sub = seq // dil
        for c in range(dil):
            for n in range(sub // BLOCK_Q):
                q_lo = c * sub + n * BLOCK_Q
                blocks.append((g, q_lo, q_lo - BLOCK_Q, 2 * BLOCK_Q) if n else (g, q_lo, q_lo, BLOCK_Q))
    ss, ps = {}, {}
    for i in range(len(blocks) + 2):
        if i >= 2:
            weighted(*ps.pop(i - 2), *blocks[i - 2])
        if 1 <= i <= len(blocks):
            ps[i - 1] = softmax(ss.pop(i - 1), blocks[i - 1][3])
        if i < len(blocks):
            ss[i] = scores(*blocks[i])

    def merge(i, _):
        rows = pl.ds(pl.multiple_of(i * chunk, chunk), chunk)
        parts = []
        for g, dil in enumerate(DILATIONS):
            if dil == 1:
                parts.append([st[g, k, rows, :] for k in range(3)])
                continue
            piece, sub = chunk // dil, seq // dil
            for c in range(dil):
                src = pl.ds(pl.multiple_of(c * sub + i * piece, piece), piece)
                dst = pl.ds(c, piece, stride=dil)
                for k in range(3):
                    stg[g - 1, k, dst, :] = st[g, k, src, :]
            parts.append([stg[g - 1, k] for k in range(3)])
        m_max = functools.reduce(jnp.maximum, [m for _, _, m in parts])
        num = den = None
        for pv, rowsum, m in parts:
            w = jnp.exp(m - m_max)
            num = w * pv if num is None else num + w * pv
            den = w * rowsum if den is None else den + w * rowsum
        o_ref[0, rows, :] = (num / den).astype(BF16)
        return 0

    lax.fori_loop(0, n_chunks, merge, 0)


def _dilated(proj3, cos3, sin3, qn2, kn2):
    b, s, _ = proj3.shape
    specs = []
    for g in range(3):
        for t in range(3):
            cb = CB_DIL + (t * 3 + g) * 4
            specs.append(pl.BlockSpec((1, s, LANES), lambda bi, j, cb=cb: (bi, 0, cb + j)))
    tab = pl.BlockSpec((1, s, LANES), lambda bi, j: (bi, 0, 0))
    vec = pl.BlockSpec((1, LANES), lambda bi, j: (0, 0))
    chunk = 2 * BLOCK_Q
    return pl.pallas_call(
        functools.partial(_dil_kernel, seq=s),
        out_shape=jax.ShapeDtypeStruct((b, s, DIL_WIDTH), BF16),
        grid=(b, DIL_WIDTH // LANES),
        in_specs=specs + [tab, tab, vec, vec],
        out_specs=pl.BlockSpec((1, s, LANES), lambda bi, j: (bi, 0, j)),
        scratch_shapes=[pltpu.VMEM((3, s, LANES), BF16)] * 5
        + [pltpu.VMEM((3, 3, s, LANES), F32)]
        + [pltpu.VMEM((2, chunk, LANES), F32)] * 3
        + [pltpu.VMEM((2, 3, chunk, LANES), F32)],
        compiler_params=_cparams(("parallel", "arbitrary")),
        name="dilated",
    )(*([proj3] * 9), cos3, sin3, qn2, kn2)


def _sb_kernel(q_ref, k_ref, v_ref, o_ref, qs, ks, vs, *, seq):
    nb = seq // BLOCK_Q
    qi = lax.broadcasted_iota(jnp.int32, (BLOCK_Q, 2 * BLOCK_Q), 0)
    kj = lax.broadcasted_iota(jnp.int32, (BLOCK_Q, 2 * BLOCK_Q), 1) % BLOCK_Q
    causal = kj < qi
    uj = lax.broadcasted_iota(jnp.int32, (2 * BLOCK_Q, 2 * BLOCK_Q), 0) % BLOCK_Q
    us = lax.broadcasted_iota(jnp.int32, (2 * BLOCK_Q, 2 * BLOCK_Q), 1)
    suffix = jnp.where((uj > us) | (us >= BLOCK_Q), 1.0, 0.0).astype(BF16)
    h0 = _head_masks(BLOCK_Q)

    def prep(i, _):
        rows = pl.ds(pl.multiple_of(i * BLOCK_Q, BLOCK_Q), BLOCK_Q)
        qs[rows, :] = q_ref[0, rows, :] * jnp.asarray(HEAD_DIM ** -0.5, BF16)
        k = k_ref[0, rows, :]
        v = v_ref[0, rows, :]
        zero = jnp.zeros_like(k)
        ks[i, :BLOCK_Q, :] = jnp.where(h0, k, zero)
        ks[i, BLOCK_Q:, :] = jnp.where(h0, zero, k)
        vs[i, :BLOCK_Q, :] = jnp.where(h0, v, zero)
        vs[i, BLOCK_Q:, :] = jnp.where(h0, zero, v)
        return 0

    lax.fori_loop(0, nb, prep, 0)

    def scores(rb, kb):
        return _dot_nt(qs[pl.ds(rb * BLOCK_Q, BLOCK_Q), :], ks[kb])

    def log_weights(z, diag):
        neg = jnp.minimum(z, 0.0)
        sp = jnp.log(1.0 + jnp.exp(-jnp.abs(z)))
        log_beta = neg - sp
        log_1m = (neg - z) - sp
        if diag:
            log_1m = jnp.where(causal, log_1m, 0.0)
        hi, lo = _split_bf16(log_1m)
        lhs = jnp.concatenate([jnp.concatenate([hi[:, sl], lo[:, sl]], axis=1)
                               for sl in (slice(0, BLOCK_Q), slice(BLOCK_Q, 2 * BLOCK_Q))], axis=0)
        su = _dot(lhs, suffix)
        w = log_beta + jnp.concatenate([su[:BLOCK_Q, :BLOCK_Q], su[BLOCK_Q:, :BLOCK_Q]], axis=1)
        tot = jnp.concatenate([su[:BLOCK_Q, BLOCK_Q:], su[BLOCK_Q:, BLOCK_Q:]], axis=1)
        return w, tot

    def accumulate(state, w, tot, rb, kb):
        diag = kb == rb
        c, acc = (None, None) if diag else state
        a = jnp.exp(w if diag else w + c)
        if diag:
            a = jnp.where(causal, a, 0.0)
        pv = _dot(a.astype(BF16), vs[kb])
        acc = pv if diag else acc + pv
        if kb == 0:
            o_ref[0, pl.ds(rb * BLOCK_Q, BLOCK_Q), :] = acc.astype(BF16)
        elif diag:
            c = tot
        else:
            c = c + tot
        return c, acc

    pairs = [(rb, kb) for rb in range(nb) for kb in range(rb, -1, -1)]
    zs, wts, state = {}, {}, None
    for i in range(len(pairs) + 2):
        if i >= 2:
            state = accumulate(state, *wts.pop(i - 2), *pairs[i - 2])
        if 1 <= i <= len(pairs):
            rb, kb = pairs[i - 1]
            wts[i - 1] = log_weights(zs.pop(i - 1), kb == rb)
        if i < len(pairs):
            zs[i] = scores(*pairs[i])


def _stick_breaking(proj3):
    b, s, _ = proj3.shape
    nb = s // BLOCK_Q
    specs = [pl.BlockSpec((1, s, LANES), lambda bi, j, cb=CB_SB + t * 4: (bi, 0, cb + j)) for t in range(3)]
    return pl.pallas_call(
        functools.partial(_sb_kernel, seq=s),
        out_shape=jax.ShapeDtypeStruct((b, s, SB_WIDTH), BF16),
        grid=(b, SB_WIDTH // LANES),
        in_specs=specs,
        out_specs=pl.BlockSpec((1, s, LANES), lambda bi, j: (bi, 0, j)),
        scratch_shapes=[pltpu.VMEM((s, LANES), BF16)] + [pltpu.VMEM((nb, 2 * BLOCK_Q, LANES), BF16)] * 2,
        compiler_params=_cparams(("parallel", "arbitrary")),
        name="stickbrk",
    )(proj3, proj3, proj3)


def _memkv_kernel(mem_ref, g_ref, w_ref, kn_ref, k_ref, v_ref):
    x = mem_ref[0]
    ms = jnp.mean(x * x, axis=-1, keepdims=True)
    h = (x * lax.rsqrt(ms + EPS) * g_ref[...]).astype(BF16)
    kv = _dot(h, w_ref[...])
    for hd in range(MEM_HEADS):
        sl = slice(hd * MEM_HEAD_DIM, (hd + 1) * MEM_HEAD_DIM)
        kh = kv[:, sl]
        msk = jnp.mean(kh * kh, axis=-1, keepdims=True)
        k_ref[0, :, sl] = (kh * lax.rsqrt(msk + EPS) * kn_ref[...]).astype(BF16)
    v_ref[0] = kv[:, MEM_WIDTH:].astype(BF16)


def _mem_kv(mem, gain, w_bf16, kn):
    b = mem.shape[0]
    out = jax.ShapeDtypeStruct((b, MEM_LEN, MEM_WIDTH), BF16)
    blk = pl.BlockSpec((1, MEM_LEN, MEM_WIDTH), lambda i: (i, 0, 0))
    return pl.pallas_call(
        _memkv_kernel,
        out_shape=(out, out),
        grid=(b,),
        in_specs=[
            pl.BlockSpec((1, MEM_LEN, D_MODEL), lambda i: (i, 0, 0)),
            pl.BlockSpec((1, D_MODEL), lambda i: (0, 0)),
            pl.BlockSpec((D_MODEL, 2 * MEM_WIDTH), lambda i: (0, 0)),
            pl.BlockSpec((1, MEM_HEAD_DIM), lambda i: (0, 0)),
        ],
        out_specs=(blk, blk),
        compiler_params=_cparams(("parallel",)),
        name="mem_kv",
    )(mem, gain, w_bf16, kn)


def _memattn_kernel(q_ref, k_ref, v_ref, qn_ref, o_ref):
    q = q_ref[0].astype(F32)
    ms = jnp.mean(q * q, axis=-1, keepdims=True)
    qn = (q * lax.rsqrt(ms + EPS) * qn_ref[...]).astype(BF16)
    s = _dot_nt(qn, k_ref[0]) * (MEM_HEAD_DIM ** -0.5)
    m = jnp.max(s, axis=-1, keepdims=True)
    p = jnp.exp(s - m)
    den = jnp.sum(p, axis=-1, keepdims=True)
    o_ref[0] = (_dot(p.astype(BF16), v_ref[0]) / den).astype(BF16)


def _mem_attn(proj3, k, v, qn, tq=1024):
    b, s, _ = proj3.shape
    kv = pl.BlockSpec((1, MEM_LEN, MEM_HEAD_DIM), lambda bi, h, i: (bi, 0, h))
    return pl.pallas_call(
        _memattn_kernel,
        out_shape=jax.ShapeDtypeStruct((b, s, MEM_WIDTH), BF16),
        grid=(b, MEM_HEADS, s // tq),
        in_specs=[
            pl.BlockSpec((1, tq, MEM_HEAD_DIM), lambda bi, h, i: (bi, i, CB_MEM + h)),
            kv, kv,
            pl.BlockSpec((1, MEM_HEAD_DIM), lambda bi, h, i: (0, 0)),
        ],
        out_specs=pl.BlockSpec((1, tq, MEM_HEAD_DIM), lambda bi, h, i: (bi, i, h)),
        compiler_params=_cparams(("parallel", "parallel", "arbitrary")),
        name="mem_attn",
    )(proj3, k, v, qn)


def _merge_kernel(x_ref, gl_ref, bg_ref, od_ref, os_ref, om_ref, wd_ref, ws_ref, wm_ref, wo_ref,
                  nf_ref, wr_ref, br_ref, x2_ref, r_ref):
    merged = None
    for i, (o_ref, w_ref) in enumerate(((od_ref, wd_ref), (os_ref, ws_ref), (om_ref, wm_ref))):
        sl = slice(i * D_MODEL, (i + 1) * D_MODEL)
        gate = jax.nn.sigmoid(gl_ref[:, sl].astype(F32) + bg_ref[:, sl])
        term = gate * _dot(o_ref[...], w_ref[...])
        merged = term if merged is None else merged + term
    x2 = x_ref[...] + _dot(merged.astype(BF16), wo_ref[...])
    x2_ref[:, :D_MODEL] = x2

    ms = jnp.mean(x2 * x2, axis=-1, keepdims=True)
    hn = x2 * lax.rsqrt(ms + EPS) * nf_ref[...]
    h_hi, h_lo = _split_bf16(hn)
    w_hi, w_lo = _split_bf16(wr_ref[...])
    lt = (_dot(h_hi, w_hi) + _dot(h_lo, w_hi) + _dot(h_hi, w_lo)).T + br_ref[...]
    row = lambda i: lt[i:i + 1, :]
    first_max = lambda v, mx: jnp.where(v[0] == mx, 0, jnp.where(v[1] == mx, 1, jnp.where(v[2] == mx, 2, 3)))
    gl = [row(i) for i in range(N_GROUPS)]
    gmax = functools.reduce(jnp.maximum, gl)
    g_top = 1.0 / functools.reduce(lambda a, c: a + c, [jnp.exp(v - gmax) for v in gl])
    g_idx = first_max(gl, gmax)
    el = []
    for e in range(EXPERTS_PER_GROUP):
        v = row(N_GROUPS + 3 * EXPERTS_PER_GROUP + e)
        for g in range(N_GROUPS - 2, -1, -1):
            v = jnp.where(g_idx == g, row(N_GROUPS + g * EXPERTS_PER_GROUP + e), v)
        el.append(v)
    emax = functools.reduce(jnp.maximum, el)
    a_idx = first_max(el, emax)
    rest = [jnp.where(a_idx == e, -jnp.inf, el[e]) for e in range(EXPERTS_PER_GROUP)]
    rmax = functools.reduce(jnp.maximum, rest)
    b_idx = first_max(rest, rmax)
    ratio = jnp.exp(rmax - emax)
    w_a = g_top / (1.0 + ratio)
    w_b = g_top * ratio / (1.0 + ratio)
    lo = jnp.minimum(a_idx, b_idx)
    hi = jnp.maximum(a_idx, b_idx)
    pair = jnp.where(lo == 0, hi - 1, jnp.where(lo == 1, hi + 1, 5))
    cls = g_idx * N_PAIRS + pair
    a_first = a_idx < b_idx
    tm = lt.shape[1]
    ri = lax.broadcasted_iota(jnp.int32, (8, tm), 0)
    info = jnp.where(ri == 0, jnp.where(a_first, w_a, w_b),
                     jnp.where(ri == 1, jnp.where(a_first, w_b, w_a), jnp.where(ri == 2, cls.astype(F32), 0.0)))
    r_ref[...] = info
    x2_ref[:, D_MODEL:] = jnp.concatenate([info, jnp.zeros((LANES - 8, tm), F32)], axis=0).T


def _merge(x2d, proj, b_gate, o_dil, o_sb, o_mem, w_o_dil, w_o_sb, w_o_mem, w_out, norm_ffn, w_rt, b_rt, tm=512):
    n = x2d.shape[0]
    row = lambda w: pl.BlockSpec((tm, w), lambda i: (i, 0))
    full = lambda a, bdim: pl.BlockSpec((a, bdim), lambda i: (0, 0))
    return pl.pallas_call(
        _merge_kernel,
        out_shape=(jax.ShapeDtypeStruct((n, X2_COLS), F32), jax.ShapeDtypeStruct((8, n), F32)),
        grid=(n // tm,),
        in_specs=[
            row(D_MODEL), row(GATE_COLS), full(1, GATE_COLS),
            row(DIL_WIDTH), row(SB_WIDTH), row(MEM_WIDTH),
            full(DIL_WIDTH, D_MODEL), full(SB_WIDTH, D_MODEL), full(MEM_WIDTH, D_MODEL), full(D_MODEL, D_MODEL),
            full(1, D_MODEL), full(D_MODEL, LANES), full(LANES, 1),
        ],
        out_specs=(row(X2_COLS), pl.BlockSpec((8, tm), lambda i: (0, i))),
        compiler_params=_cparams(("parallel",)),
        name="merge",
    )(x2d, proj, b_gate, o_dil, o_sb, o_mem, w_o_dil, w_o_sb, w_o_mem, w_out, norm_ffn, w_rt, b_rt)


def _moe_kernel(elo_ref, ehi_ref, nv_ref, src_ref,
                x_hbm, nf_ref, wgl_ref, wul_ref, wdl_ref, wgh_ref, wuh_ref, wdh_ref,
                out_hbm, xbuf, obuf, gsem, ssem, *, tile, n_tiles):
    i = pl.program_id(0)
    slot = i % 2

    def row_in(tok, r, s):
        return pltpu.make_async_copy(x_hbm.at[pl.ds(tok, 1), :], xbuf.at[s, pl.ds(r, 1), :], gsem.at[s])

    def row_out(tok, r, s):
        return pltpu.make_async_copy(obuf.at[s, pl.ds(r, 1), :], out_hbm.at[pl.ds(tok, 1), :], ssem.at[s])

    def rows_out(t, s, count, wait):
        def one(r):
            if wait:
                row_out(0, r, s).wait()
            else:
                row_out(src_ref[t * tile + r], r, s).start()

        def chunk(c, _):
            for u in range(DMA_UNROLL):
                one(c * DMA_UNROLL + u)
            return 0

        def single(r, _):
            one(r)
            return 0

        full = count // DMA_UNROLL
        lax.fori_loop(0, full, chunk, 0)
        lax.fori_loop(full * DMA_UNROLL, count, single, 0)

    def gather(t, s):
        @pl.when(nv_ref[t] > 0)
        def _():
            def chunk(c, _):
                for u in range(DMA_UNROLL):
                    r = c * DMA_UNROLL + u
                    row_in(src_ref[t * tile + r], r, s).start()
                return 0
            lax.fori_loop(0, tile // DMA_UNROLL, chunk, 0)

    def drain_out(t, s):
        rows_out(t, s, nv_ref[t], True)

    @pl.when(i == 0)
    def _():
        gather(0, 0)

    @pl.when(i + 1 < n_tiles)
    def _():
        gather(i + 1, 1 - slot)

    @pl.when(i >= 2)
    def _():
        drain_out(i - 2, slot)

    nv = nv_ref[i]

    @pl.when(nv > 0)
    def _():
        for r in range(tile):
            row_in(0, r, slot).wait()

        x = xbuf[slot, :, :D_MODEL]
        cw = xbuf[slot, :, D_MODEL:]
        ms = jnp.mean(x * x, axis=-1, keepdims=True)
        hn = (x * lax.rsqrt(ms + EPS) * nf_ref[...]).astype(BF16)
        moe = None
        for c, (wg, wu, wd) in enumerate(((wgl_ref, wul_ref, wdl_ref), (wgh_ref, wuh_ref, wdh_ref))):
            gate = _dot(hn, wg[0])
            up = _dot(hn, wu[0])
            act = (gate * jax.nn.sigmoid(gate) * up).astype(BF16)
            term = cw[:, c:c + 1] * _dot(act, wd[0])
            moe = term if moe is None else moe + term
        obuf[slot] = x + moe

        rows_out(i, slot, nv, False)

    @pl.when(i == n_tiles - 1)
    def _():
        @pl.when(i >= 1)
        def _():
            drain_out(i - 1, 1 - slot)
        drain_out(i, slot)


def _moe(x2e, norm_ffn, wg, wu, wd, tile_elo, tile_ehi, tile_nv, src, tile=MOE_TILE):
    n = x2e.shape[0]
    n_tiles = tile_nv.shape[0]
    w_in = lambda sel: pl.BlockSpec((1, D_MODEL, D_EXPERT), lambda i, elo, ehi, nv, s: ((elo, ehi)[sel][i], 0, 0))
    w_dn = lambda sel: pl.BlockSpec((1, D_EXPERT, D_MODEL), lambda i, elo, ehi, nv, s: ((elo, ehi)[sel][i], 0, 0))
    grid_spec = pltpu.PrefetchScalarGridSpec(
        num_scalar_prefetch=4,
        grid=(n_tiles,),
        in_specs=[
            pl.BlockSpec(memory_space=pl.ANY),
            pl.BlockSpec((1, D_MODEL), lambda i, *_: (0, 0)),
            w_in(0), w_in(0), w_dn(0), w_in(1), w_in(1), w_dn(1),
        ],
        out_specs=pl.BlockSpec(memory_space=pl.ANY),
        scratch_shapes=[
            pltpu.VMEM((2, tile, X2_COLS), F32),
            pltpu.VMEM((2, tile, D_MODEL), F32),
            pltpu.SemaphoreType.DMA((2,)),
            pltpu.SemaphoreType.DMA((2,)),
        ],
    )
    return pl.pallas_call(
        functools.partial(_moe_kernel, tile=tile, n_tiles=n_tiles),
        out_shape=jax.ShapeDtypeStruct((n, D_MODEL), F32),
        grid_spec=grid_spec,
        compiler_params=_cparams(("arbitrary",)),
        name="moe",
    )(tile_elo, tile_ehi, tile_nv, src, x2e, norm_ffn, wg, wu, wd, wg, wu, wd)


def _moe_plan(cls, n, tile):
    n_tiles = (n + N_CLASSES * (tile - 1)) // tile
    order = jnp.argsort(cls, stable=True).astype(jnp.int32)
    counts = jnp.sum((cls[:, None] == jnp.arange(N_CLASSES, dtype=jnp.int32)[None, :]).astype(jnp.int32), axis=0)
    tiles_per = (counts + tile - 1) // tile
    tile_end = jnp.cumsum(tiles_per)
    tile_start = tile_end - tiles_per
    tok_start = jnp.cumsum(counts) - counts
    t = jnp.arange(n_tiles, dtype=jnp.int32)
    tile_cls = jnp.minimum(jnp.sum((t[:, None] >= tile_end[None, :]).astype(jnp.int32), axis=1), N_CLASSES - 1)
    rank0 = (t - tile_start[tile_cls]) * tile
    tile_nv = jnp.clip(counts[tile_cls] - rank0, 0, tile).astype(jnp.int32)
    tile_nv = jnp.where(t < tile_end[-1], tile_nv, 0)
    r = jnp.arange(tile, dtype=jnp.int32)
    pos = tok_start[tile_cls][:, None] + rank0[:, None] + r[None, :]
    valid = r[None, :] < tile_nv[:, None]
    src = jnp.where(valid, order[jnp.clip(pos, 0, n - 1)], 0).reshape(-1).astype(jnp.int32)
    pair_lo = jnp.array([0, 0, 0, 1, 1, 2], jnp.int32)
    pair_hi = jnp.array([1, 2, 3, 2, 3, 3], jnp.int32)
    grp = tile_cls // N_PAIRS
    tile_elo = (grp * EXPERTS_PER_GROUP + pair_lo[tile_cls % N_PAIRS]).astype(jnp.int32)
    tile_ehi = (grp * EXPERTS_PER_GROUP + pair_hi[tile_cls % N_PAIRS]).astype(jnp.int32)
    return tile_elo, tile_ehi, tile_nv, src


def kernel(x, mem, positions, norm_mix, norm_mem, w_in, b_gate, qn_dil, kn_dil, qn_mem, kn_mem, w_mem_kv,
           w_o_dil, w_o_sb, w_o_mem, w_out, norm_ffn, w_router_group, b_router_group, w_router_expert,
           b_router_expert, w_exp_gate, w_exp_up, w_exp_down):
    b, s, d = x.shape
    n = b * s
    assert d == D_MODEL and w_in.shape == (1, D_MODEL, IN_COLS) and s % (BLOCK_Q * DILATIONS[-1]) == 0
    off_g = IN_COLS - GATE_COLS
    x2d = x.reshape(n, d)

    w_in_p = jnp.concatenate([w_in[0][:, off_g:], w_in[0][:, :off_g]], axis=1).astype(BF16)
    proj = _in_proj(x2d, norm_mix, w_in_p)
    proj3 = proj.reshape(b, s, IN_COLS)

    inv_freq = ROPE_THETA ** (-jnp.arange(0, HEAD_DIM, 2, dtype=F32) / HEAD_DIM)
    invf = jnp.tile(inv_freq, LANES // (HEAD_DIM // 2))[None, :]
    sign = jnp.tile(jnp.concatenate([-jnp.ones(HEAD_DIM // 2, F32), jnp.ones(HEAD_DIM // 2, F32)]), 2)[None, :]
    pos_b = jnp.broadcast_to(positions.astype(F32).reshape(n, 1), (n, LANES))
    cos_t, sin_t = _rope_tables(pos_b, invf, sign)
    o_dil = _dilated(proj3, cos_t.reshape(b, s, LANES), sin_t.reshape(b, s, LANES),
                     jnp.tile(qn_dil, (1, 2)), jnp.tile(kn_dil, (1, 2)))

    o_sb = _stick_breaking(proj3)

    k_mem, v_mem = _mem_kv(mem, norm_mem, w_mem_kv[0].astype(BF16), kn_mem)
    o_mem = _mem_attn(proj3, k_mem, v_mem, qn_mem)

    pad = LANES - N_GROUPS - N_EXPERTS
    w_rt = jnp.concatenate([w_router_group[0], w_router_expert[0], jnp.zeros((D_MODEL, pad), F32)], axis=1)
    b_rt = jnp.concatenate([b_router_group[0], b_router_expert[0], jnp.zeros((pad,), F32)])[:, None]
    x2e, route = _merge(x2d, proj, b_gate, o_dil.reshape(n, DIL_WIDTH), o_sb.reshape(n, SB_WIDTH),
                        o_mem.reshape(n, MEM_WIDTH), w_o_dil[0].astype(BF16), w_o_sb[0].astype(BF16),
                        w_o_mem[0].astype(BF16), w_out[0].astype(BF16), norm_ffn, w_rt, b_rt)

    tile_elo, tile_ehi, tile_nv, src = _moe_plan(route[2].astype(jnp.int32), n, MOE_TILE)
    out = _moe(x2e, norm_ffn, w_exp_gate[0].astype(BF16), w_exp_up[0].astype(BF16),
               w_exp_down[0].astype(BF16), tile_elo, tile_ehi, tile_nv, src)
    return out.reshape(b, s, d)
```

```python
import functools

import jax
import jax.numpy as jnp
from jax import lax
from jax.experimental import pallas as pl
from jax.experimental.pallas import tpu as pltpu

F32 = jnp.float32
BF16 = jnp.bfloat16

D_MODEL = 1024
HEAD_DIM = 64
LANES = 128
BLOCK_Q = 128
ROPE_THETA = 10000.0
EPS = 1e-6
DILATIONS = (1, 4, 16)
DIL_WIDTH = 512
SB_WIDTH = 512
MEM_WIDTH = 512
MEM_HEADS = 4
MEM_HEAD_DIM = 128
MEM_LEN = 256
GATE_COLS = 3 * D_MODEL
A_COLS = 9 * DIL_WIDTH
B_COLS = 3 * SB_WIDTH
IN_COLS = A_COLS + B_COLS + MEM_WIDTH + GATE_COLS
CB_DIL = GATE_COLS // LANES
CB_SB = CB_DIL + A_COLS // LANES
CB_MEM = CB_SB + B_COLS // LANES
N_GROUPS = 4
EXPERTS_PER_GROUP = 4
N_EXPERTS = 16
D_EXPERT = 512
N_PAIRS = 6
N_CLASSES = N_GROUPS * N_PAIRS
MOE_TILE = 256
DMA_UNROLL = 8
X2_ROWS = 16
NEG = -1e30
VMEM_LIMIT = 56 * 1024 * 1024


def _cparams(sem):
    return pltpu.CompilerParams(dimension_semantics=sem, vmem_limit_bytes=VMEM_LIMIT)


def _dot(a, b):
    return jnp.dot(a, b, preferred_element_type=F32)


def _dot_nt(a, b):
    return lax.dot_general(a, b, (((1,), (1,)), ((), ())), preferred_element_type=F32)


def _split_bf16(a):
    hi = a.astype(BF16)
    return hi, (a - hi.astype(F32)).astype(BF16)


def _head_masks(rows):
    lane = lax.broadcasted_iota(jnp.int32, (rows, LANES), 1)
    return lane < HEAD_DIM


def _inproj_kernel(x_ref, g_ref, w_ref, o_ref, *, chunk):
    x = x_ref[...]
    ms = jnp.mean(x * x, axis=-1, keepdims=True)
    h = (x * lax.rsqrt(ms + EPS) * g_ref[...]).astype(BF16)
    for c in range(IN_COLS // chunk):
        sl = slice(c * chunk, (c + 1) * chunk)
        o_ref[:, sl] = _dot(h, w_ref[:, sl]).astype(BF16)


def _in_proj(x2d, gain, w_bf16, tm=256, chunk=512):
    n = x2d.shape[0]
    return pl.pallas_call(
        functools.partial(_inproj_kernel, chunk=chunk),
        out_shape=jax.ShapeDtypeStruct((n, IN_COLS), BF16),
        grid=(n // tm,),
        in_specs=[
            pl.BlockSpec((tm, D_MODEL), lambda i: (i, 0)),
            pl.BlockSpec((1, D_MODEL), lambda i: (0, 0)),
            pl.BlockSpec((D_MODEL, IN_COLS), lambda i: (0, 0), pipeline_mode=pl.Buffered(1)),
        ],
        out_specs=pl.BlockSpec((tm, IN_COLS), lambda i: (i, 0)),
        compiler_params=_cparams(("parallel",)),
        name="in_proj",
    )(x2d, gain, w_bf16)


def _rope_kernel(pos_ref, invf_ref, sign_ref, cos_ref, sin_ref):
    ang = pos_ref[...] * invf_ref[...]
    cos_ref[...] = jnp.cos(ang)
    sin_ref[...] = jnp.sin(ang) * sign_ref[...]


def _rope_tables(pos_b, invf, sign, tm=2048):
    n = pos_b.shape[0]
    row = pl.BlockSpec((tm, LANES), lambda i: (i, 0))
    vec = pl.BlockSpec((1, LANES), lambda i: (0, 0))
    return pl.pallas_call(
        _rope_kernel,
        out_shape=(jax.ShapeDtypeStruct((n, LANES), F32),) * 2,
        grid=(n // tm,),
        in_specs=[row, vec, vec],
        out_specs=(row, row),
        compiler_params=_cparams(("parallel",)),
        name="rope_tab",
    )(pos_b, invf, sign)


def _dil_kernel(q0, k0, v0, q1, k1, v1, q2, k2, v2, cos_ref, sin_ref, qn_ref, kn_ref, o_ref,
                qp, km0, km1, vm0, vm1, st, stq, stk, stv, stg, *, seq):
    qkv = ((q0, k0, v0), (q1, k1, v1), (q2, k2, v2))
    chunk = 2 * BLOCK_Q
    n_chunks = seq // chunk

    gi = lax.broadcasted_iota(jnp.int32, (LANES, LANES), 0) // HEAD_DIM
    gj = lax.broadcasted_iota(jnp.int32, (LANES, LANES), 1) // HEAD_DIM
    head_mean = jnp.where(gi == gj, 1.0 / HEAD_DIM, 0.0).astype(BF16)
    lane_c = lax.broadcasted_iota(jnp.int32, (chunk, LANES), 1)
    first_half = (lane_c % HEAD_DIM) < (HEAD_DIM // 2)

    def norm_rope(t_ref, gain, rows, scale):
        t = t_ref[0, rows, :].astype(F32)
        hi, lo = _split_bf16(t * t)
        tn = t * lax.rsqrt(_dot(hi, head_mean) + _dot(lo, head_mean) + EPS) * gain
        rot = jnp.where(first_half, pltpu.roll(tn, LANES - HEAD_DIM // 2, 1), pltpu.roll(tn, HEAD_DIM // 2, 1))
        out = tn * cos_ref[0, rows, :] + rot * sin_ref[0, rows, :]
        return out * scale if scale != 1.0 else out

    def put(g, dst, qb, kb, vb):
        h = _head_masks(kb.shape[0])
        zero = jnp.zeros_like(kb)
        qp[g, dst, :] = qb
        km0[g, dst, :] = jnp.where(h, kb, zero)
        km1[g, dst, :] = jnp.where(h, zero, kb)
        vm0[g, dst, :] = jnp.where(h, vb, zero)
        vm1[g, dst, :] = jnp.where(h, zero, vb)

    def prep(i, _):
        rows = pl.ds(pl.multiple_of(i * chunk, chunk), chunk)
        for g, dil in enumerate(DILATIONS):
            q_ref, k_ref, v_ref = qkv[g]
            qn = norm_rope(q_ref, qn_ref[...], rows, HEAD_DIM ** -0.5)
            kn = norm_rope(k_ref, kn_ref[...], rows, 1.0)
            if dil == 1:
                put(g, rows, qn.astype(BF16), kn.astype(BF16), v_ref[0, rows, :])
                continue
            stq[g - 1] = qn
            stk[g - 1] = kn
            stv[g - 1] = v_ref[0, rows, :].astype(F32)
            piece, sub = chunk // dil, seq // dil
            for c in range(dil):
                src = pl.ds(c, piece, stride=dil)
                dst = pl.ds(pl.multiple_of(c * sub + i * piece, piece), piece)
                put(g, dst, stq[g - 1, src, :].astype(BF16), stk[g - 1, src, :].astype(BF16),
                    stv[g - 1, src, :].astype(BF16))
        return 0

    lax.fori_loop(0, n_chunks, prep, 0)

    qi = lax.broadcasted_iota(jnp.int32, (BLOCK_Q, BLOCK_Q), 0)
    kj = lax.broadcasted_iota(jnp.int32, (BLOCK_Q, BLOCK_Q), 1)
    bias_cur = jnp.where(kj <= qi, 0.0, NEG).astype(F32)
    bias_prev = jnp.where(kj >= qi, 0.0, NEG).astype(F32)
    bias = {BLOCK_Q: jnp.concatenate([bias_cur] * 2, axis=1),
            2 * BLOCK_Q: jnp.concatenate([bias_prev, bias_cur] * 2, axis=1)}
    hq = _head_masks(BLOCK_Q)
    ones = {nk: jnp.where(_head_masks(nk), 1.0, 0.0).astype(BF16) for nk in bias}

    def scores(g, q_lo, k_lo, nk):
        keys = jnp.concatenate([km0[g, k_lo:k_lo + nk, :], km1[g, k_lo:k_lo + nk, :]], axis=0)
        return _dot_nt(qp[g, q_lo:q_lo + BLOCK_Q, :], keys)

    def softmax(s, nk):
        s = s + bias[nk]
        mx = [jnp.max(s[:, h * nk:(h + 1) * nk], axis=-1, keepdims=True) for h in range(2)]
        m_all = jnp.concatenate([jnp.broadcast_to(m, (BLOCK_Q, nk)) for m in mx], axis=1)
        return jnp.exp(s - m_all).astype(BF16), jnp.where(hq, mx[0], mx[1])

    def weighted(p, mt, g, q_lo, k_lo, nk):
        rhs = jnp.concatenate([
            jnp.concatenate([vm0[g, k_lo:k_lo + nk, :], ones[nk]], axis=1),
            jnp.concatenate([vm1[g, k_lo:k_lo + nk, :], 1.0 - ones[nk]], axis=1)], axis=0)
        pv = _dot(p, rhs)
        st[g, 0, q_lo:q_lo + BLOCK_Q, :] = pv[:, :LANES]
        st[g, 1, q_lo:q_lo + BLOCK_Q, :] = pv[:, LANES:]
        st[g, 2, q_lo:q_lo + BLOCK_Q, :] = mt

    blocks = []
    for g, dil in enumerate(DILATIONS):
        sub = seq // dil
        for c in range(dil):
            for n in range(sub // BLOCK_Q):
                q_lo = c * sub + n * BLOCK_Q
                blocks.append((g, q_lo, q_lo - BLOCK_Q, 2 * BLOCK_Q) if n else (g, q_lo, q_lo, BLOCK_Q))
    ss, ps = {}, {}
    for i in range(len(blocks) + 2):
        if i >= 2:
            weighted(*ps.pop(i - 2), *blocks[i - 2])
        if 1 <= i <= len(blocks):
            ps[i - 1] = softmax(ss.pop(i - 1), blocks[i - 1][3])
        if i < len(blocks):
            ss[i] = scores(*blocks[i])

    def merge(i, _):
        rows = pl.ds(pl.multiple_of(i * chunk, chunk), chunk)
        parts = []
        for g, dil in enumerate(DILATIONS):
            if dil == 1:
                parts.append([st[g, k, rows, :] for k in range(3)])
                continue
            piece, sub = chunk // dil, seq // dil
            for c in range(dil):
                src = pl.ds(pl.multiple_of(c * sub + i * piece, piece), piece)
                dst = pl.ds(c, piece, stride=dil)
                for k in range(3):
                    stg[g - 1, k, dst, :] = st[g, k, src, :]
            parts.append([stg[g - 1, k] for k in range(3)])
        m_max = functools.reduce(jnp.maximum, [m for _, _, m in parts])
        num = den = None
        for pv, rowsum, m in parts:
            w = jnp.exp(m - m_max)
            num = w * pv if num is None else num + w * pv
            den = w * rowsum if den is None else den + w * rowsum
        o_ref[0, rows, :] = (num / den).astype(BF16)
        return 0

    lax.fori_loop(0, n_chunks, merge, 0)


def _dilated(proj3, cos3, sin3, qn2, kn2):
    b, s, _ = proj3.shape
    specs = []
    for g in range(3):
        for t in range(3):
            cb = CB_DIL + (t * 3 + g) * 4
            specs.append(pl.BlockSpec((1, s, LANES), lambda bi, j, cb=cb: (bi, 0, cb + j)))
    tab = pl.BlockSpec((1, s, LANES), lambda bi, j: (bi, 0, 0))
    vec = pl.BlockSpec((1, LANES), lambda bi, j: (0, 0))
    chunk = 2 * BLOCK_Q
    return pl.pallas_call(
        functools.partial(_dil_kernel, seq=s),
        out_shape=jax.ShapeDtypeStruct((b, s, DIL_WIDTH), BF16),
        grid=(b, DIL_WIDTH // LANES),
        in_specs=specs + [tab, tab, vec, vec],
        out_specs=pl.BlockSpec((1, s, LANES), lambda bi, j: (bi, 0, j)),
        scratch_shapes=[pltpu.VMEM((3, s, LANES), BF16)] * 5
        + [pltpu.VMEM((3, 3, s, LANES), F32)]
        + [pltpu.VMEM((2, chunk, LANES), F32)] * 3
        + [pltpu.VMEM((2, 3, chunk, LANES), F32)],
        compiler_params=_cparams(("parallel", "arbitrary")),
        name="dilated",
    )(*([proj3] * 9), cos3, sin3, qn2, kn2)


def _sb_kernel(q_ref, k_ref, v_ref, o_ref, qs, ks, vs, *, seq):
    nb = seq // BLOCK_Q
    qi = lax.broadcasted_iota(jnp.int32, (BLOCK_Q, 2 * BLOCK_Q), 0)
    kj = lax.broadcasted_iota(jnp.int32, (BLOCK_Q, 2 * BLOCK_Q), 1) % BLOCK_Q
    causal = kj < qi
    uj = lax.broadcasted_iota(jnp.int32, (2 * BLOCK_Q, 2 * BLOCK_Q), 0) % BLOCK_Q
    us = lax.broadcasted_iota(jnp.int32, (2 * BLOCK_Q, 2 * BLOCK_Q), 1)
    suffix = jnp.where((uj > us) | (us >= BLOCK_Q), 1.0, 0.0).astype(BF16)
    h0 = _head_masks(BLOCK_Q)

    def prep(i, _):
        rows = pl.ds(pl.multiple_of(i * BLOCK_Q, BLOCK_Q), BLOCK_Q)
        qs[rows, :] = q_ref[0, rows, :] * jnp.asarray(HEAD_DIM ** -0.5, BF16)
        k = k_ref[0, rows, :]
        v = v_ref[0, rows, :]
        zero = jnp.zeros_like(k)
        ks[i, :BLOCK_Q, :] = jnp.where(h0, k, zero)
        ks[i, BLOCK_Q:, :] = jnp.where(h0, zero, k)
        vs[i, :BLOCK_Q, :] = jnp.where(h0, v, zero)
        vs[i, BLOCK_Q:, :] = jnp.where(h0, zero, v)
        return 0

    lax.fori_loop(0, nb, prep, 0)

    def scores(rb, kb):
        return _dot_nt(qs[pl.ds(rb * BLOCK_Q, BLOCK_Q), :], ks[kb])

    def log_weights(z, diag):
        neg = jnp.minimum(z, 0.0)
        sp = jnp.log(1.0 + jnp.exp(-jnp.abs(z)))
        log_beta = neg - sp
        log_1m = (neg - z) - sp
        if diag:
            log_1m = jnp.where(causal, log_1m, 0.0)
        hi, lo = _split_bf16(log_1m)
        lhs = jnp.concatenate([jnp.concatenate([hi[:, sl], lo[:, sl]], axis=1)
                               for sl in (slice(0, BLOCK_Q), slice(BLOCK_Q, 2 * BLOCK_Q))], axis=0)
        su = _dot(lhs, suffix)
        w = log_beta + jnp.concatenate([su[:BLOCK_Q, :BLOCK_Q], su[BLOCK_Q:, :BLOCK_Q]], axis=1)
        tot = jnp.concatenate([su[:BLOCK_Q, BLOCK_Q:], su[BLOCK_Q:, BLOCK_Q:]], axis=1)
        return w, tot

    def accumulate(state, w, tot, rb, kb):
        diag = kb == rb
        c, acc = (None, None) if diag else state
        a = jnp.exp(w if diag else w + c)
        if diag:
            a = jnp.where(causal, a, 0.0)
        pv = _dot(a.astype(BF16), vs[kb])
        acc = pv if diag else acc + pv
        if kb == 0:
            o_ref[0, pl.ds(rb * BLOCK_Q, BLOCK_Q), :] = acc.astype(BF16)
        elif diag:
            c = tot
        else:
            c = c + tot
        return c, acc

    pairs = [(rb, kb) for rb in range(nb) for kb in range(rb, -1, -1)]
    zs, wts, state = {}, {}, None
    for i in range(len(pairs) + 2):
        if i >= 2:
            state = accumulate(state, *wts.pop(i - 2), *pairs[i - 2])
        if 1 <= i <= len(pairs):
            rb, kb = pairs[i - 1]
            wts[i - 1] = log_weights(zs.pop(i - 1), kb == rb)
        if i < len(pairs):
            zs[i] = scores(*pairs[i])


def _stick_breaking(proj3):
    b, s, _ = proj3.shape
    nb = s // BLOCK_Q
    specs = [pl.BlockSpec((1, s, LANES), lambda bi, j, cb=CB_SB + t * 4: (bi, 0, cb + j)) for t in range(3)]
    return pl.pallas_call(
        functools.partial(_sb_kernel, seq=s),
        out_shape=jax.ShapeDtypeStruct((b, s, SB_WIDTH), BF16),
        grid=(b, SB_WIDTH // LANES),
        in_specs=specs,
        out_specs=pl.BlockSpec((1, s, LANES), lambda bi, j: (bi, 0, j)),
        scratch_shapes=[pltpu.VMEM((s, LANES), BF16)] + [pltpu.VMEM((nb, 2 * BLOCK_Q, LANES), BF16)] * 2,
        compiler_params=_cparams(("parallel", "arbitrary")),
        name="stickbrk",
    )(proj3, proj3, proj3)


def _memkv_kernel(mem_ref, g_ref, w_ref, kn_ref, k_ref, v_ref):
    x = mem_ref[0]
    ms = jnp.mean(x * x, axis=-1, keepdims=True)
    h = (x * lax.rsqrt(ms + EPS) * g_ref[...]).astype(BF16)
    kv = _dot(h, w_ref[...])
    for hd in range(MEM_HEADS):
        sl = slice(hd * MEM_HEAD_DIM, (hd + 1) * MEM_HEAD_DIM)
        kh = kv[:, sl]
        msk = jnp.mean(kh * kh, axis=-1, keepdims=True)
        k_ref[0, :, sl] = (kh * lax.rsqrt(msk + EPS) * kn_ref[...]).astype(BF16)
    v_ref[0] = kv[:, MEM_WIDTH:].astype(BF16)


def _mem_kv(mem, gain, w_bf16, kn):
    b = mem.shape[0]
    out = jax.ShapeDtypeStruct((b, MEM_LEN, MEM_WIDTH), BF16)
    blk = pl.BlockSpec((1, MEM_LEN, MEM_WIDTH), lambda i: (i, 0, 0))
    return pl.pallas_call(
        _memkv_kernel,
        out_shape=(out, out),
        grid=(b,),
        in_specs=[
            pl.BlockSpec((1, MEM_LEN, D_MODEL), lambda i: (i, 0, 0)),
            pl.BlockSpec((1, D_MODEL), lambda i: (0, 0)),
            pl.BlockSpec((D_MODEL, 2 * MEM_WIDTH), lambda i: (0, 0)),
            pl.BlockSpec((1, MEM_HEAD_DIM), lambda i: (0, 0)),
        ],
        out_specs=(blk, blk),
        compiler_params=_cparams(("parallel",)),
        name="mem_kv",
    )(mem, gain, w_bf16, kn)


def _memattn_kernel(q_ref, k_ref, v_ref, qn_ref, o_ref):
    q = q_ref[0].astype(F32)
    ms = jnp.mean(q * q, axis=-1, keepdims=True)
    qn = (q * lax.rsqrt(ms + EPS) * qn_ref[...]).astype(BF16)
    s = _dot_nt(qn, k_ref[0]) * (MEM_HEAD_DIM ** -0.5)
    m = jnp.max(s, axis=-1, keepdims=True)
    p = jnp.exp(s - m)
    den = jnp.sum(p, axis=-1, keepdims=True)
    o_ref[0] = (_dot(p.astype(BF16), v_ref[0]) / den).astype(BF16)


def _mem_attn(proj3, k, v, qn, tq=1024):
    b, s, _ = proj3.shape
    kv = pl.BlockSpec((1, MEM_LEN, MEM_HEAD_DIM), lambda bi, h, i: (bi, 0, h))
    return pl.pallas_call(
        _memattn_kernel,
        out_shape=jax.ShapeDtypeStruct((b, s, MEM_WIDTH), BF16),
        grid=(b, MEM_HEADS, s // tq),
        in_specs=[
            pl.BlockSpec((1, tq, MEM_HEAD_DIM), lambda bi, h, i: (bi, i, CB_MEM + h)),
            kv, kv,
            pl.BlockSpec((1, MEM_HEAD_DIM), lambda bi, h, i: (0, 0)),
        ],
        out_specs=pl.BlockSpec((1, tq, MEM_HEAD_DIM), lambda bi, h, i: (bi, i, h)),
        compiler_params=_cparams(("parallel", "parallel", "arbitrary")),
        name="mem_attn",
    )(proj3, k, v, qn)


def _merge_kernel(x_ref, gl_ref, bg_ref, od_ref, os_ref, om_ref, wd_ref, ws_ref, wm_ref, wo_ref,
                  nf_ref, wr_ref, br_ref, x2_ref, r_ref):
    merged = None
    for i, (o_ref, w_ref) in enumerate(((od_ref, wd_ref), (os_ref, ws_ref), (om_ref, wm_ref))):
        sl = slice(i * D_MODEL, (i + 1) * D_MODEL)
        gate = jax.nn.sigmoid(gl_ref[:, sl].astype(F32) + bg_ref[:, sl])
        term = gate * _dot(o_ref[...], w_ref[...])
        merged = term if merged is None else merged + term
    x2 = x_ref[...] + _dot(merged.astype(BF16), wo_ref[...])
    tm = x2.shape[0]
    for j in range(D_MODEL // LANES):
        x2_ref[pl.ds(j, tm, stride=X2_ROWS), :] = x2[:, j * LANES:(j + 1) * LANES]

    ms = jnp.mean(x2 * x2, axis=-1, keepdims=True)
    hn = x2 * lax.rsqrt(ms + EPS) * nf_ref[...]
    h_hi, h_lo = _split_bf16(hn)
    w_hi, w_lo = _split_bf16(wr_ref[...])
    lt = (_dot(h_hi, w_hi) + _dot(h_lo, w_hi) + _dot(h_hi, w_lo)).T + br_ref[...]
    row = lambda i: lt[i:i + 1, :]
    first_max = lambda v, mx: jnp.where(v[0] == mx, 0, jnp.where(v[1] == mx, 1, jnp.where(v[2] == mx, 2, 3)))
    gl = [row(i) for i in range(N_GROUPS)]
    gmax = functools.reduce(jnp.maximum, gl)
    g_top = 1.0 / functools.reduce(lambda a, c: a + c, [jnp.exp(v - gmax) for v in gl])
    g_idx = first_max(gl, gmax)
    el = []
    for e in range(EXPERTS_PER_GROUP):
        v = row(N_GROUPS + 3 * EXPERTS_PER_GROUP + e)
        for g in range(N_GROUPS - 2, -1, -1):
            v = jnp.where(g_idx == g, row(N_GROUPS + g * EXPERTS_PER_GROUP + e), v)
        el.append(v)
    emax = functools.reduce(jnp.maximum, el)
    a_idx = first_max(el, emax)
    rest = [jnp.where(a_idx == e, -jnp.inf, el[e]) for e in range(EXPERTS_PER_GROUP)]
    rmax = functools.reduce(jnp.maximum, rest)
    b_idx = first_max(rest, rmax)
    ratio = jnp.exp(rmax - emax)
    w_a = g_top / (1.0 + ratio)
    w_b = g_top * ratio / (1.0 + ratio)
    lo = jnp.minimum(a_idx, b_idx)
    hi = jnp.maximum(a_idx, b_idx)
    pair = jnp.where(lo == 0, hi - 1, jnp.where(lo == 1, hi + 1, 5))
    cls = g_idx * N_PAIRS + pair
    a_first = a_idx < b_idx
    tm = lt.shape[1]
    ri = lax.broadcasted_iota(jnp.int32, (8, tm), 0)
    info = jnp.where(ri == 0, jnp.where(a_first, w_a, w_b),
                     jnp.where(ri == 1, jnp.where(a_first, w_b, w_a), jnp.where(ri == 2, cls.astype(F32), 0.0)))
    r_ref[...] = info
    x2_ref[pl.ds(D_MODEL // LANES, tm, stride=X2_ROWS), :] = jnp.concatenate(
        [info, jnp.zeros((LANES - 8, tm), F32)], axis=0).T
    for j in range(D_MODEL // LANES + 1, X2_ROWS):
        x2_ref[pl.ds(j, tm, stride=X2_ROWS), :] = jnp.zeros((tm, LANES), F32)


def _merge(x2d, proj, b_gate, o_dil, o_sb, o_mem, w_o_dil, w_o_sb, w_o_mem, w_out, norm_ffn, w_rt, b_rt, tm=512):
    n = x2d.shape[0]
    row = lambda w: pl.BlockSpec((tm, w), lambda i: (i, 0))
    full = lambda a, bdim: pl.BlockSpec((a, bdim), lambda i: (0, 0))
    return pl.pallas_call(
        _merge_kernel,
        out_shape=(jax.ShapeDtypeStruct((n * X2_ROWS, LANES), F32), jax.ShapeDtypeStruct((8, n), F32)),
        grid=(n // tm,),
        in_specs=[
            row(D_MODEL), row(GATE_COLS), full(1, GATE_COLS),
            row(DIL_WIDTH), row(SB_WIDTH), row(MEM_WIDTH),
            full(DIL_WIDTH, D_MODEL), full(SB_WIDTH, D_MODEL), full(MEM_WIDTH, D_MODEL), full(D_MODEL, D_MODEL),
            full(1, D_MODEL), full(D_MODEL, LANES), full(LANES, 1),
        ],
        out_specs=(pl.BlockSpec((tm * X2_ROWS, LANES), lambda i: (i, 0)), pl.BlockSpec((8, tm), lambda i: (0, i))),
        compiler_params=_cparams(("parallel",)),
        name="merge",
    )(x2d, proj, b_gate, o_dil, o_sb, o_mem, w_o_dil, w_o_sb, w_o_mem, w_out, norm_ffn, w_rt, b_rt)


def _moe_kernel(elo_ref, ehi_ref, nv_ref, src_ref,
                x_hbm, nf_ref, wgl_ref, wul_ref, wdl_ref, wgh_ref, wuh_ref, wdh_ref,
                out_hbm, xbuf, obuf, gsem, ssem, *, tile, n_tiles):
    i = pl.program_id(0)
    slot = i % 2

    def row_in(tok, r, s):
        return pltpu.make_async_copy(x_hbm.at[pl.ds(pl.multiple_of(tok * X2_ROWS, X2_ROWS), X2_ROWS), :],
                                     xbuf.at[s, pl.ds(r * X2_ROWS, X2_ROWS), :], gsem.at[s])

    def row_out(tok, r, s):
        return pltpu.make_async_copy(obuf.at[s, pl.ds(r, 1), :], out_hbm.at[pl.ds(tok, 1), :], ssem.at[s])

    def rows_out(t, s, count, wait):
        def one(r):
            if wait:
                row_out(0, r, s).wait()
            else:
                row_out(src_ref[t * tile + r], r, s).start()

        def chunk(c, _):
            for u in range(DMA_UNROLL):
                one(c * DMA_UNROLL + u)
            return 0

        def single(r, _):
            one(r)
            return 0

        full = count // DMA_UNROLL
        lax.fori_loop(0, full, chunk, 0)
        lax.fori_loop(full * DMA_UNROLL, count, single, 0)

    def drain_out(t, s):
        rows_out(t, s, nv_ref[t], True)

    def wait_gather():
        for r in range(tile):
            row_in(0, r, slot).wait()

    nv = nv_ref[i]

    @pl.when((i == 0) & (nv > 0))
    def _():
        def chunk(c, _):
            for u in range(DMA_UNROLL):
                r = c * DMA_UNROLL + u
                row_in(src_ref[r], r, 0).start()
            return 0
        lax.fori_loop(0, tile // DMA_UNROLL, chunk, 0)

    @pl.when(i >= 2)
    def _():
        drain_out(i - 2, slot)

    prev_nv = jnp.where(i > 0, nv_ref[jnp.maximum(i - 1, 0)], 0)
    prev_inline = (nv > 0) & (prev_nv == tile)

    @pl.when((prev_nv > 0) & jnp.logical_not(prev_inline))
    def _():
        rows_out(i - 1, 1 - slot, prev_nv, False)

    @pl.when((nv == 0) & (prev_nv > 0))
    def _():
        wait_gather()

    def expert_block(scatter_prev):
        wait_gather()
        for r in range(tile):
            row_in(src_ref[(i + 1) * tile + r], r, 1 - slot).start()
        if scatter_prev:
            for r in range(tile):
                row_out(src_ref[(i - 1) * tile + r], r, 1 - slot).start()

        x = jnp.concatenate([xbuf[slot, pl.ds(j, tile, stride=X2_ROWS), :] for j in range(D_MODEL // LANES)], axis=1)
        cw = xbuf[slot, pl.ds(D_MODEL // LANES, tile, stride=X2_ROWS), :]
        ms = jnp.mean(x * x, axis=-1, keepdims=True)
        hn = (x * lax.rsqrt(ms + EPS) * nf_ref[...]).astype(BF16)
        moe = None
        for c, (wg, wu, wd) in enumerate(((wgl_ref, wul_ref, wdl_ref), (wgh_ref, wuh_ref, wdh_ref))):
            gate = _dot(hn, wg[0])
            up = _dot(hn, wu[0])
            act = (gate * jax.nn.sigmoid(gate) * up).astype(BF16)
            term = cw[:, c:c + 1] * _dot(act, wd[0])
            moe = term if moe is None else moe + term
        obuf[slot] = x + moe

    @pl.when(prev_inline)
    def _():
        expert_block(True)

    @pl.when((nv > 0) & (prev_nv != tile))
    def _():
        expert_block(False)

    @pl.when((i == n_tiles - 1) & (i >= 1))
    def _():
        drain_out(i - 1, 1 - slot)


def _moe(x2e, norm_ffn, wg, wu, wd, tile_elo, tile_ehi, tile_nv, src, tile=MOE_TILE):
    n = x2e.shape[0] // X2_ROWS
    n_tiles = tile_nv.shape[0]
    w_in = lambda sel: pl.BlockSpec((1, D_MODEL, D_EXPERT), lambda i, elo, ehi, nv, s: ((elo, ehi)[sel][i], 0, 0))
    w_dn = lambda sel: pl.BlockSpec((1, D_EXPERT, D_MODEL), lambda i, elo, ehi, nv, s: ((elo, ehi)[sel][i], 0, 0))
    grid_spec = pltpu.PrefetchScalarGridSpec(
        num_scalar_prefetch=4,
        grid=(n_tiles,),
        in_specs=[
            pl.BlockSpec(memory_space=pl.ANY),
            pl.BlockSpec((1, D_MODEL), lambda i, *_: (0, 0)),
            w_in(0), w_in(0), w_dn(0), w_in(1), w_in(1), w_dn(1),
        ],
        out_specs=pl.BlockSpec(memory_space=pl.ANY),
        scratch_shapes=[
            pltpu.VMEM((2, tile * X2_ROWS, LANES), F32),
            pltpu.VMEM((2, tile, D_MODEL), F32),
            pltpu.SemaphoreType.DMA((2,)),
            pltpu.SemaphoreType.DMA((2,)),
        ],
    )
    return pl.pallas_call(
        functools.partial(_moe_kernel, tile=tile, n_tiles=n_tiles),
        out_shape=jax.ShapeDtypeStruct((n, D_MODEL), F32),
        grid_spec=grid_spec,
        compiler_params=_cparams(("arbitrary",)),
        name="moe",
    )(tile_elo, tile_ehi, tile_nv, src, x2e, norm_ffn, wg, wu, wd, wg, wu, wd)


def _moe_plan(cls, n, tile):
    n_tiles = (n + N_CLASSES * (tile - 1)) // tile + 1
    order = jnp.argsort(cls, stable=True).astype(jnp.int32)
    counts = jnp.sum((cls[:, None] == jnp.arange(N_CLASSES, dtype=jnp.int32)[None, :]).astype(jnp.int32), axis=0)
    tiles_per = (counts + tile - 1) // tile
    tile_end = jnp.cumsum(tiles_per)
    tile_start = tile_end - tiles_per
    tok_start = jnp.cumsum(counts) - counts
    t = jnp.arange(n_tiles, dtype=jnp.int32)
    tile_cls = jnp.minimum(jnp.sum((t[:, None] >= tile_end[None, :]).astype(jnp.int32), axis=1), N_CLASSES - 1)
    rank0 = (t - tile_start[tile_cls]) * tile
    tile_nv = jnp.clip(counts[tile_cls] - rank0, 0, tile).astype(jnp.int32)
    tile_nv = jnp.where(t < tile_end[-1], tile_nv, 0)
    r = jnp.arange(tile, dtype=jnp.int32)
    pos = tok_start[tile_cls][:, None] + rank0[:, None] + r[None, :]
    valid = r[None, :] < tile_nv[:, None]
    src = jnp.where(valid, order[jnp.clip(pos, 0, n - 1)], 0).reshape(-1).astype(jnp.int32)
    pair_lo = jnp.array([0, 0, 0, 1, 1, 2], jnp.int32)
    pair_hi = jnp.array([1, 2, 3, 2, 3, 3], jnp.int32)
    grp = tile_cls // N_PAIRS
    tile_elo = (grp * EXPERTS_PER_GROUP + pair_lo[tile_cls % N_PAIRS]).astype(jnp.int32)
    tile_ehi = (grp * EXPERTS_PER_GROUP + pair_hi[tile_cls % N_PAIRS]).astype(jnp.int32)
    return tile_elo, tile_ehi, tile_nv, src


def kernel(x, mem, positions, norm_mix, norm_mem, w_in, b_gate, qn_dil, kn_dil, qn_mem, kn_mem, w_mem_kv,
           w_o_dil, w_o_sb, w_o_mem, w_out, norm_ffn, w_router_group, b_router_group, w_router_expert,
           b_router_expert, w_exp_gate, w_exp_up, w_exp_down):
    b, s, d = x.shape
    n = b * s
    assert d == D_MODEL and w_in.shape == (1, D_MODEL, IN_COLS) and s % (BLOCK_Q * DILATIONS[-1]) == 0
    off_g = IN_COLS - GATE_COLS
    x2d = x.reshape(n, d)

    w_in_p = jnp.concatenate([w_in[0][:, off_g:], w_in[0][:, :off_g]], axis=1).astype(BF16)
    proj = _in_proj(x2d, norm_mix, w_in_p)
    proj3 = proj.reshape(b, s, IN_COLS)

    inv_freq = ROPE_THETA ** (-jnp.arange(0, HEAD_DIM, 2, dtype=F32) / HEAD_DIM)
    invf = jnp.tile(inv_freq, LANES // (HEAD_DIM // 2))[None, :]
    sign = jnp.tile(jnp.concatenate([-jnp.ones(HEAD_DIM // 2, F32), jnp.ones(HEAD_DIM // 2, F32)]), 2)[None, :]
    pos_b = jnp.broadcast_to(positions.astype(F32).reshape(n, 1), (n, LANES))
    cos_t, sin_t = _rope_tables(pos_b, invf, sign)
    o_dil = _dilated(proj3, cos_t.reshape(b, s, LANES), sin_t.reshape(b, s, LANES),
                     jnp.tile(qn_dil, (1, 2)), jnp.tile(kn_dil, (1, 2)))

    o_sb = _stick_breaking(proj3)

    k_mem, v_mem = _mem_kv(mem, norm_mem, w_mem_kv[0].astype(BF16), kn_mem)
    o_mem = _mem_attn(proj3, k_mem, v_mem, qn_mem)

    pad = LANES - N_GROUPS - N_EXPERTS
    w_rt = jnp.concatenate([w_router_group[0], w_router_expert[0], jnp.zeros((D_MODEL, pad), F32)], axis=1)
    b_rt = jnp.concatenate([b_router_group[0], b_router_expert[0], jnp.zeros((pad,), F32)])[:, None]
    x2e, route = _merge(x2d, proj, b_gate, o_dil.reshape(n, DIL_WIDTH), o_sb.reshape(n, SB_WIDTH),
                        o_mem.reshape(n, MEM_WIDTH), w_o_dil[0].astype(BF16), w_o_sb[0].astype(BF16),
                        w_o_mem[0].astype(BF16), w_out[0].astype(BF16), norm_ffn, w_rt, b_rt)

    tile_elo, tile_ehi, tile_nv, src = _moe_plan(route[2].astype(jnp.int32), n, MOE_TILE)
    out = _moe(x2e, norm_ffn, w_exp_gate[0].astype(BF16), w_exp_up[0].astype(BF16),
               w_exp_down[0].astype(BF16), tile_elo, tile_ehi, tile_nv, src)
    return out.reshape(b, s, d)
```

```python
import functools

import jax
import jax.numpy as jnp
from jax import lax
from jax.experimental import pallas as pl
from jax.experimental.pallas import tpu as pltpu

F32 = jnp.float32
BF16 = jnp.bfloat16

D_MODEL = 1024
HEAD_DIM = 64
LANES = 128
BLOCK_Q = 128
ROPE_THETA = 10000.0
EPS = 1e-6
DILATIONS = (1, 4, 16)
DIL_WIDTH = 512
SB_WIDTH = 512
MEM_WIDTH = 512
MEM_HEADS = 4
MEM_HEAD_DIM = 128
MEM_LEN = 256
GATE_COLS = 3 * D_MODEL
A_COLS = 9 * DIL_WIDTH
B_COLS = 3 * SB_WIDTH
IN_COLS = A_COLS + B_COLS + MEM_WIDTH + GATE_COLS
CB_DIL = GATE_COLS // LANES
CB_SB = CB_DIL + A_COLS // LANES
CB_MEM = CB_SB + B_COLS // LANES
N_GROUPS = 4
EXPERTS_PER_GROUP = 4
N_EXPERTS = 16
D_EXPERT = 512
N_PAIRS = 6
N_CLASSES = N_GROUPS * N_PAIRS
MOE_TILE = 256
STAGE_GAP = 2
DMA_UNROLL = 8
X2_ROWS = 16
NEG = -1e30
VMEM_LIMIT = 56 * 1024 * 1024


def _cparams(sem):
    return pltpu.CompilerParams(dimension_semantics=sem, vmem_limit_bytes=VMEM_LIMIT)


def _dot(a, b):
    return jnp.dot(a, b, preferred_element_type=F32)


def _dot_nt(a, b):
    return lax.dot_general(a, b, (((1,), (1,)), ((), ())), preferred_element_type=F32)


def _split_bf16(a):
    hi = a.astype(BF16)
    return hi, (a - hi.astype(F32)).astype(BF16)


def _head_masks(rows):
    lane = lax.broadcasted_iota(jnp.int32, (rows, LANES), 1)
    return lane < HEAD_DIM


def _inproj_kernel(x_ref, g_ref, w_ref, o_ref, *, chunk):
    x = x_ref[...]
    ms = jnp.mean(x * x, axis=-1, keepdims=True)
    h = (x * lax.rsqrt(ms + EPS) * g_ref[...]).astype(BF16)
    for c in range(IN_COLS // chunk):
        sl = slice(c * chunk, (c + 1) * chunk)
        o_ref[:, sl] = _dot(h, w_ref[:, sl]).astype(BF16)


def _in_proj(x2d, gain, w_bf16, tm=256, chunk=512):
    n = x2d.shape[0]
    return pl.pallas_call(
        functools.partial(_inproj_kernel, chunk=chunk),
        out_shape=jax.ShapeDtypeStruct((n, IN_COLS), BF16),
        grid=(n // tm,),
        in_specs=[
            pl.BlockSpec((tm, D_MODEL), lambda i: (i, 0)),
            pl.BlockSpec((1, D_MODEL), lambda i: (0, 0)),
            pl.BlockSpec((D_MODEL, IN_COLS), lambda i: (0, 0), pipeline_mode=pl.Buffered(1)),
        ],
        out_specs=pl.BlockSpec((tm, IN_COLS), lambda i: (i, 0)),
        compiler_params=_cparams(("parallel",)),
        name="in_proj",
    )(x2d, gain, w_bf16)


def _rope_kernel(pos_ref, invf_ref, sign_ref, cos_ref, sin_ref):
    ang = pos_ref[...] * invf_ref[...]
    cos_ref[...] = jnp.cos(ang)
    sin_ref[...] = jnp.sin(ang) * sign_ref[...]


def _rope_tables(pos_b, invf, sign, tm=2048):
    n = pos_b.shape[0]
    row = pl.BlockSpec((tm, LANES), lambda i: (i, 0))
    vec = pl.BlockSpec((1, LANES), lambda i: (0, 0))
    return pl.pallas_call(
        _rope_kernel,
        out_shape=(jax.ShapeDtypeStruct((n, LANES), F32),) * 2,
        grid=(n // tm,),
        in_specs=[row, vec, vec],
        out_specs=(row, row),
        compiler_params=_cparams(("parallel",)),
        name="rope_tab",
    )(pos_b, invf, sign)


def _dil_kernel(q0, k0, v0, q1, k1, v1, q2, k2, v2, cos_ref, sin_ref, qn_ref, kn_ref, o_ref,
                qp, km0, km1, vm0, vm1, st, stq, stk, stv, stg, *, seq):
    qkv = ((q0, k0, v0), (q1, k1, v1), (q2, k2, v2))
    chunk = 2 * BLOCK_Q
    n_chunks = seq // chunk

    gi = lax.broadcasted_iota(jnp.int32, (LANES, LANES), 0) // HEAD_DIM
    gj = lax.broadcasted_iota(jnp.int32, (LANES, LANES), 1) // HEAD_DIM
    head_mean = jnp.where(gi == gj, 1.0 / HEAD_DIM, 0.0).astype(BF16)
    lane_c = lax.broadcasted_iota(jnp.int32, (chunk, LANES), 1)
    first_half = (lane_c % HEAD_DIM) < (HEAD_DIM // 2)

    def norm_rope(t_ref, gain, rows, scale):
        t = t_ref[0, rows, :].astype(F32)
        hi, lo = _split_bf16(t * t)
        tn = t * lax.rsqrt(_dot(hi, head_mean) + _dot(lo, head_mean) + EPS) * gain
        rot = jnp.where(first_half, pltpu.roll(tn, LANES - HEAD_DIM // 2, 1), pltpu.roll(tn, HEAD_DIM // 2, 1))
        out = tn * cos_ref[0, rows, :] + rot * sin_ref[0, rows, :]
        return out * scale if scale != 1.0 else out

    def put(g, dst, qb, kb, vb):
        h = _head_masks(kb.shape[0])
        zero = jnp.zeros_like(kb)
        qp[g, dst, :] = qb
        km0[g, dst, :] = jnp.where(h, kb, zero)
        km1[g, dst, :] = jnp.where(h, zero, kb)
        vm0[g, dst, :] = jnp.where(h, vb, zero)
        vm1[g, dst, :] = jnp.where(h, zero, vb)

    def prep(i, _):
        rows = pl.ds(pl.multiple_of(i * chunk, chunk), chunk)
        for g, dil in enumerate(DILATIONS):
            q_ref, k_ref, v_ref = qkv[g]
            qn = norm_rope(q_ref, qn_ref[...], rows, HEAD_DIM ** -0.5)
            kn = norm_rope(k_ref, kn_ref[...], rows, 1.0)
            if dil == 1:
                put(g, rows, qn.astype(BF16), kn.astype(BF16), v_ref[0, rows, :])
                continue
            stq[g - 1] = qn
            stk[g - 1] = kn
            stv[g - 1] = v_ref[0, rows, :].astype(F32)
            piece, sub = chunk // dil, seq // dil
            for c in range(dil):
                src = pl.ds(c, piece, stride=dil)
                dst = pl.ds(pl.multiple_of(c * sub + i * piece, piece), piece)
                put(g, dst, stq[g - 1, src, :].astype(BF16), stk[g - 1, src, :].astype(BF16),
                    stv[g - 1, src, :].astype(BF16))
        return 0

    lax.fori_loop(0, n_chunks, prep, 0)

    qi = lax.broadcasted_iota(jnp.int32, (BLOCK_Q, BLOCK_Q), 0)
    kj = lax.broadcasted_iota(jnp.int32, (BLOCK_Q, BLOCK_Q), 1)
    bias_cur = jnp.where(kj <= qi, 0.0, NEG).astype(F32)
    bias_prev = jnp.where(kj >= qi, 0.0, NEG).astype(F32)
    bias = {BLOCK_Q: jnp.concatenate([bias_cur] * 2, axis=1),
            2 * BLOCK_Q: jnp.concatenate([bias_prev, bias_cur] * 2, axis=1)}
    hq = _head_masks(BLOCK_Q)
    ones = {nk: jnp.where(_head_masks(nk), 1.0, 0.0).astype(BF16) for nk in bias}

    def scores(g, q_lo, k_lo, nk):
        keys = jnp.concatenate([km0[g, k_lo:k_lo + nk, :], km1[g, k_lo:k_lo + nk, :]], axis=0)
        return _dot_nt(qp[g, q_lo:q_lo + BLOCK_Q, :], keys)

    def softmax(s, nk):
        s = s + bias[nk]
        mx = [jnp.max(s[:, h * nk:(h + 1) * nk], axis=-1, keepdims=True) for h in range(2)]
        m_all = jnp.concatenate([jnp.broadcast_to(m, (BLOCK_Q, nk)) for m in mx], axis=1)
        return jnp.exp(s - m_all).astype(BF16), jnp.where(hq, mx[0], mx[1])

    def weighted(p, mt, g, q_lo, k_lo, nk):
        rhs = jnp.concatenate([
            jnp.concatenate([vm0[g, k_lo:k_lo + nk, :], ones[nk]], axis=1),
            jnp.concatenate([vm1[g, k_lo:k_lo + nk, :], 1.0 - ones[nk]], axis=1)], axis=0)
        pv = _dot(p, rhs)
        st[g, 0, q_lo:q_lo + BLOCK_Q, :] = pv[:, :LANES]
        st[g, 1, q_lo:q_lo + BLOCK_Q, :] = pv[:, LANES:]
        st[g, 2, q_lo:q_lo + BLOCK_Q, :] = mt

    blocks = []
    for g, dil in enumerate(DILATIONS):
        sub = seq // dil
        for c in range(dil):
            for n in range(sub // BLOCK_Q):
                q_lo = c * sub + n * BLOCK_Q
                blocks.append((g, q_lo, q_lo - BLOCK_Q, 2 * BLOCK_Q) if n else (g, q_lo, q_lo, BLOCK_Q))
    ss, ps = {}, {}
    for i in range(len(blocks) + 2 * STAGE_GAP):
        if i >= 2 * STAGE_GAP:
            j = i - 2 * STAGE_GAP
            weighted(*ps.pop(j), *blocks[j])
        if STAGE_GAP <= i < len(blocks) + STAGE_GAP:
            j = i - STAGE_GAP
            ps[j] = softmax(ss.pop(j), blocks[j][3])
        if i < len(blocks):
            ss[i] = scores(*blocks[i])

    def merge(i, _):
        rows = pl.ds(pl.multiple_of(i * chunk, chunk), chunk)
        parts = []
        for g, dil in enumerate(DILATIONS):
            if dil == 1:
                parts.append([st[g, k, rows, :] for k in range(3)])
                continue
            piece, sub = chunk // dil, seq // dil
            for c in range(dil):
                src = pl.ds(pl.multiple_of(c * sub + i * piece, piece), piece)
                dst = pl.ds(c, piece, stride=dil)
                for k in range(3):
                    stg[g - 1, k, dst, :] = st[g, k, src, :]
            parts.append([stg[g - 1, k] for k in range(3)])
        m_max = functools.reduce(jnp.maximum, [m for _, _, m in parts])
        num = den = None
        for pv, rowsum, m in parts:
            w = jnp.exp(m - m_max)
            num = w * pv if num is None else num + w * pv
            den = w * rowsum if den is None else den + w * rowsum
        o_ref[0, rows, :] = (num / den).astype(BF16)
        return 0

    lax.fori_loop(0, n_chunks, merge, 0)


def _dilated(proj3, cos3, sin3, qn2, kn2):
    b, s, _ = proj3.shape
    specs = []
    for g in range(3):
        for t in range(3):
            cb = CB_DIL + (t * 3 + g) * 4
            specs.append(pl.BlockSpec((1, s, LANES), lambda bi, j, cb=cb: (bi, 0, cb + j)))
    tab = pl.BlockSpec((1, s, LANES), lambda bi, j: (bi, 0, 0))
    vec = pl.BlockSpec((1, LANES), lambda bi, j: (0, 0))
    chunk = 2 * BLOCK_Q
    return pl.pallas_call(
        functools.partial(_dil_kernel, seq=s),
        out_shape=jax.ShapeDtypeStruct((b, s, DIL_WIDTH), BF16),
        grid=(b, DIL_WIDTH // LANES),
        in_specs=specs + [tab, tab, vec, vec],
        out_specs=pl.BlockSpec((1, s, LANES), lambda bi, j: (bi, 0, j)),
        scratch_shapes=[pltpu.VMEM((3, s, LANES), BF16)] * 5
        + [pltpu.VMEM((3, 3, s, LANES), F32)]
        + [pltpu.VMEM((2, chunk, LANES), F32)] * 3
        + [pltpu.VMEM((2, 3, chunk, LANES), F32)],
        compiler_params=_cparams(("parallel", "arbitrary")),
        name="dilated",
    )(*([proj3] * 9), cos3, sin3, qn2, kn2)


def _sb_kernel(q_ref, k_ref, v_ref, o_ref, qs, ks, vs, *, seq):
    nb = seq // BLOCK_Q
    qi = lax.broadcasted_iota(jnp.int32, (BLOCK_Q, 2 * BLOCK_Q), 0)
    kj = lax.broadcasted_iota(jnp.int32, (BLOCK_Q, 2 * BLOCK_Q), 1) % BLOCK_Q
    causal = kj < qi
    uj = lax.broadcasted_iota(jnp.int32, (2 * BLOCK_Q, 2 * BLOCK_Q), 0) % BLOCK_Q
    us = lax.broadcasted_iota(jnp.int32, (2 * BLOCK_Q, 2 * BLOCK_Q), 1)
    suffix = jnp.where((uj > us) | (us >= BLOCK_Q), 1.0, 0.0).astype(BF16)
    h0 = _head_masks(BLOCK_Q)

    def prep(i, _):
        rows = pl.ds(pl.multiple_of(i * BLOCK_Q, BLOCK_Q), BLOCK_Q)
        qs[rows, :] = q_ref[0, rows, :] * jnp.asarray(HEAD_DIM ** -0.5, BF16)
        k = k_ref[0, rows, :]
        v = v_ref[0, rows, :]
        zero = jnp.zeros_like(k)
        ks[i, :BLOCK_Q, :] = jnp.where(h0, k, zero)
        ks[i, BLOCK_Q:, :] = jnp.where(h0, zero, k)
        vs[i, :BLOCK_Q, :] = jnp.where(h0, v, zero)
        vs[i, BLOCK_Q:, :] = jnp.where(h0, zero, v)
        return 0

    lax.fori_loop(0, nb, prep, 0)

    def scores(rb, kb):
        return _dot_nt(qs[pl.ds(rb * BLOCK_Q, BLOCK_Q), :], ks[kb])

    def log_weights(z, diag):
        neg = jnp.minimum(z, 0.0)
        sp = jnp.log(1.0 + jnp.exp(-jnp.abs(z)))
        log_beta = neg - sp
        log_1m = (neg - z) - sp
        if diag:
            log_1m = jnp.where(causal, log_1m, 0.0)
        hi, lo = _split_bf16(log_1m)
        lhs = jnp.concatenate([jnp.concatenate([hi[:, sl], lo[:, sl]], axis=1)
                               for sl in (slice(0, BLOCK_Q), slice(BLOCK_Q, 2 * BLOCK_Q))], axis=0)
        su = _dot(lhs, suffix)
        w = log_beta + jnp.concatenate([su[:BLOCK_Q, :BLOCK_Q], su[BLOCK_Q:, :BLOCK_Q]], axis=1)
        tot = jnp.concatenate([su[:BLOCK_Q, BLOCK_Q:], su[BLOCK_Q:, BLOCK_Q:]], axis=1)
        return w, tot

    def accumulate(state, w, tot, rb, kb):
        diag = kb == rb
        c, acc = (None, None) if diag else state
        a = jnp.exp(w if diag else w + c)
        if diag:
            a = jnp.where(causal, a, 0.0)
        pv = _dot(a.astype(BF16), vs[kb])
        acc = pv if diag else acc + pv
        if kb == 0:
            o_ref[0, pl.ds(rb * BLOCK_Q, BLOCK_Q), :] = acc.astype(BF16)
        elif diag:
            c = tot
        else:
            c = c + tot
        return c, acc

    pairs = [(rb, kb) for rb in range(nb) for kb in range(rb, -1, -1)]
    zs, wts, state = {}, {}, None
    for i in range(len(pairs) + 2 * STAGE_GAP):
        if i >= 2 * STAGE_GAP:
            j = i - 2 * STAGE_GAP
            state = accumulate(state, *wts.pop(j), *pairs[j])
        if STAGE_GAP <= i < len(pairs) + STAGE_GAP:
            j = i - STAGE_GAP
            wts[j] = log_weights(zs.pop(j), pairs[j][0] == pairs[j][1])
        if i < len(pairs):
            zs[i] = scores(*pairs[i])


def _stick_breaking(proj3):
    b, s, _ = proj3.shape
    nb = s // BLOCK_Q
    specs = [pl.BlockSpec((1, s, LANES), lambda bi, j, cb=CB_SB + t * 4: (bi, 0, cb + j)) for t in range(3)]
    return pl.pallas_call(
        functools.partial(_sb_kernel, seq=s),
        out_shape=jax.ShapeDtypeStruct((b, s, SB_WIDTH), BF16),
        grid=(b, SB_WIDTH // LANES),
        in_specs=specs,
        out_specs=pl.BlockSpec((1, s, LANES), lambda bi, j: (bi, 0, j)),
        scratch_shapes=[pltpu.VMEM((s, LANES), BF16)] + [pltpu.VMEM((nb, 2 * BLOCK_Q, LANES), BF16)] * 2,
        compiler_params=_cparams(("parallel", "arbitrary")),
        name="stickbrk",
    )(proj3, proj3, proj3)


def _memkv_kernel(mem_ref, g_ref, w_ref, kn_ref, k_ref, v_ref):
    x = mem_ref[0]
    ms = jnp.mean(x * x, axis=-1, keepdims=True)
    h = (x * lax.rsqrt(ms + EPS) * g_ref[...]).astype(BF16)
    kv = _dot(h, w_ref[...])
    for hd in range(MEM_HEADS):
        sl = slice(hd * MEM_HEAD_DIM, (hd + 1) * MEM_HEAD_DIM)
        kh = kv[:, sl]
        msk = jnp.mean(kh * kh, axis=-1, keepdims=True)
        k_ref[0, :, sl] = (kh * lax.rsqrt(msk + EPS) * kn_ref[...]).astype(BF16)
    v_ref[0] = kv[:, MEM_WIDTH:].astype(BF16)


def _mem_kv(mem, gain, w_bf16, kn):
    b = mem.shape[0]
    out = jax.ShapeDtypeStruct((b, MEM_LEN, MEM_WIDTH), BF16)
    blk = pl.BlockSpec((1, MEM_LEN, MEM_WIDTH), lambda i: (i, 0, 0))
    return pl.pallas_call(
        _memkv_kernel,
        out_shape=(out, out),
        grid=(b,),
        in_specs=[
            pl.BlockSpec((1, MEM_LEN, D_MODEL), lambda i: (i, 0, 0)),
            pl.BlockSpec((1, D_MODEL), lambda i: (0, 0)),
            pl.BlockSpec((D_MODEL, 2 * MEM_WIDTH), lambda i: (0, 0)),
            pl.BlockSpec((1, MEM_HEAD_DIM), lambda i: (0, 0)),
        ],
        out_specs=(blk, blk),
        compiler_params=_cparams(("parallel",)),
        name="mem_kv",
    )(mem, gain, w_bf16, kn)


def _memattn_kernel(q_ref, k_ref, v_ref, qn_ref, o_ref):
    q = q_ref[0].astype(F32)
    ms = jnp.mean(q * q, axis=-1, keepdims=True)
    qn = (q * lax.rsqrt(ms + EPS) * qn_ref[...]).astype(BF16)
    s = _dot_nt(qn, k_ref[0]) * (MEM_HEAD_DIM ** -0.5)
    m = jnp.max(s, axis=-1, keepdims=True)
    p = jnp.exp(s - m)
    den = jnp.sum(p, axis=-1, keepdims=True)
    o_ref[0] = (_dot(p.astype(BF16), v_ref[0]) / den).astype(BF16)


def _mem_attn(proj3, k, v, qn, tq=1024):
    b, s, _ = proj3.shape
    kv = pl.BlockSpec((1, MEM_LEN, MEM_HEAD_DIM), lambda bi, h, i: (bi, 0, h))
    return pl.pallas_call(
        _memattn_kernel,
        out_shape=jax.ShapeDtypeStruct((b, s, MEM_WIDTH), BF16),
        grid=(b, MEM_HEADS, s // tq),
        in_specs=[
            pl.BlockSpec((1, tq, MEM_HEAD_DIM), lambda bi, h, i: (bi, i, CB_MEM + h)),
            kv, kv,
            pl.BlockSpec((1, MEM_HEAD_DIM), lambda bi, h, i: (0, 0)),
        ],
        out_specs=pl.BlockSpec((1, tq, MEM_HEAD_DIM), lambda bi, h, i: (bi, i, h)),
        compiler_params=_cparams(("parallel", "parallel", "arbitrary")),
        name="mem_attn",
    )(proj3, k, v, qn)


def _merge_kernel(x_ref, gl_ref, bg_ref, od_ref, os_ref, om_ref, wd_ref, ws_ref, wm_ref, wo_ref,
                  nf_ref, wr_ref, br_ref, x2_ref, r_ref):
    merged = None
    for i, (o_ref, w_ref) in enumerate(((od_ref, wd_ref), (os_ref, ws_ref), (om_ref, wm_ref))):
        sl = slice(i * D_MODEL, (i + 1) * D_MODEL)
        gate = jax.nn.sigmoid(gl_ref[:, sl].astype(F32) + bg_ref[:, sl])
        term = gate * _dot(o_ref[...], w_ref[...])
        merged = term if merged is None else merged + term
    x2 = x_ref[...] + _dot(merged.astype(BF16), wo_ref[...])
    tm = x2.shape[0]
    for j in range(D_MODEL // LANES):
        x2_ref[pl.ds(j, tm, stride=X2_ROWS), :] = x2[:, j * LANES:(j + 1) * LANES]

    ms = jnp.mean(x2 * x2, axis=-1, keepdims=True)
    hn = x2 * lax.rsqrt(ms + EPS) * nf_ref[...]
    h_hi, h_lo = _split_bf16(hn)
    w_hi, w_lo = _split_bf16(wr_ref[...])
    lt = (_dot(h_hi, w_hi) + _dot(h_lo, w_hi) + _dot(h_hi, w_lo)).T + br_ref[...]
    row = lambda i: lt[i:i + 1, :]
    first_max = lambda v, mx: jnp.where(v[0] == mx, 0, jnp.where(v[1] == mx, 1, jnp.where(v[2] == mx, 2, 3)))
    gl = [row(i) for i in range(N_GROUPS)]
    gmax = functools.reduce(jnp.maximum, gl)
    g_top = 1.0 / functools.reduce(lambda a, c: a + c, [jnp.exp(v - gmax) for v in gl])
    g_idx = first_max(gl, gmax)
    el = []
    for e in range(EXPERTS_PER_GROUP):
        v = row(N_GROUPS + 3 * EXPERTS_PER_GROUP + e)
        for g in range(N_GROUPS - 2, -1, -1):
            v = jnp.where(g_idx == g, row(N_GROUPS + g * EXPERTS_PER_GROUP + e), v)
        el.append(v)
    emax = functools.reduce(jnp.maximum, el)
    a_idx = first_max(el, emax)
    rest = [jnp.where(a_idx == e, -jnp.inf, el[e]) for e in range(EXPERTS_PER_GROUP)]
    rmax = functools.reduce(jnp.maximum, rest)
    b_idx = first_max(rest, rmax)
    ratio = jnp.exp(rmax - emax)
    w_a = g_top / (1.0 + ratio)
    w_b = g_top * ratio / (1.0 + ratio)
    lo = jnp.minimum(a_idx, b_idx)
    hi = jnp.maximum(a_idx, b_idx)
    pair = jnp.where(lo == 0, hi - 1, jnp.where(lo == 1, hi + 1, 5))
    cls = g_idx * N_PAIRS + pair
    a_first = a_idx < b_idx
    tm = lt.shape[1]
    ri = lax.broadcasted_iota(jnp.int32, (8, tm), 0)
    info = jnp.where(ri == 0, jnp.where(a_first, w_a, w_b),
                     jnp.where(ri == 1, jnp.where(a_first, w_b, w_a), jnp.where(ri == 2, cls.astype(F32), 0.0)))
    r_ref[...] = info
    x2_ref[pl.ds(D_MODEL // LANES, tm, stride=X2_ROWS), :] = jnp.concatenate(
        [info, jnp.zeros((LANES - 8, tm), F32)], axis=0).T
    for j in range(D_MODEL // LANES + 1, X2_ROWS):
        x2_ref[pl.ds(j, tm, stride=X2_ROWS), :] = jnp.zeros((tm, LANES), F32)


def _merge(x2d, proj, b_gate, o_dil, o_sb, o_mem, w_o_dil, w_o_sb, w_o_mem, w_out, norm_ffn, w_rt, b_rt, tm=512):
    n = x2d.shape[0]
    row = lambda w: pl.BlockSpec((tm, w), lambda i: (i, 0))
    full = lambda a, bdim: pl.BlockSpec((a, bdim), lambda i: (0, 0))
    return pl.pallas_call(
        _merge_kernel,
        out_shape=(jax.ShapeDtypeStruct((n * X2_ROWS, LANES), F32), jax.ShapeDtypeStruct((8, n), F32)),
        grid=(n // tm,),
        in_specs=[
            row(D_MODEL), row(GATE_COLS), full(1, GATE_COLS),
            row(DIL_WIDTH), row(SB_WIDTH), row(MEM_WIDTH),
            full(DIL_WIDTH, D_MODEL), full(SB_WIDTH, D_MODEL), full(MEM_WIDTH, D_MODEL), full(D_MODEL, D_MODEL),
            full(1, D_MODEL), full(D_MODEL, LANES), full(LANES, 1),
        ],
        out_specs=(pl.BlockSpec((tm * X2_ROWS, LANES), lambda i: (i, 0)), pl.BlockSpec((8, tm), lambda i: (0, i))),
        compiler_params=_cparams(("parallel",)),
        name="merge",
    )(x2d, proj, b_gate, o_dil, o_sb, o_mem, w_o_dil, w_o_sb, w_o_mem, w_out, norm_ffn, w_rt, b_rt)


def _moe_kernel(elo_ref, ehi_ref, nv_ref, src_ref,
                x_hbm, nf_ref, wgl_ref, wul_ref, wdl_ref, wgh_ref, wuh_ref, wdh_ref,
                out_hbm, xbuf, obuf, gsem, ssem, *, tile, n_tiles):
    i = pl.program_id(0)
    slot = i % 2

    def row_in(tok, r, s):
        return pltpu.make_async_copy(x_hbm.at[pl.ds(pl.multiple_of(tok * X2_ROWS, X2_ROWS), X2_ROWS), :],
                                     xbuf.at[s, pl.ds(r * X2_ROWS, X2_ROWS), :], gsem.at[s])

    def row_out(tok, r, s):
        return pltpu.make_async_copy(obuf.at[s, pl.ds(r, 1), :], out_hbm.at[pl.ds(tok, 1), :], ssem.at[s])

    def rows_out(t, s, count, wait):
        def one(r):
            if wait:
                row_out(0, r, s).wait()
            else:
                row_out(src_ref[t * tile + r], r, s).start()

        def chunk(c, _):
            for u in range(DMA_UNROLL):
                one(c * DMA_UNROLL + u)
            return 0

        def single(r, _):
            one(r)
            return 0

        full = count // DMA_UNROLL
        lax.fori_loop(0, full, chunk, 0)
        lax.fori_loop(full * DMA_UNROLL, count, single, 0)

    def drain_out(t, s):
        rows_out(t, s, nv_ref[t], True)

    def wait_gather():
        for r in range(tile):
            row_in(0, r, slot).wait()

    nv = nv_ref[i]

    @pl.when((i == 0) & (nv > 0))
    def _():
        def chunk(c, _):
            for u in range(DMA_UNROLL):
                r = c * DMA_UNROLL + u
                row_in(src_ref[r], r, 0).start()
            return 0
        lax.fori_loop(0, tile // DMA_UNROLL, chunk, 0)

    @pl.when(i >= 2)
    def _():
        drain_out(i - 2, slot)

    prev_nv = jnp.where(i > 0, nv_ref[jnp.maximum(i - 1, 0)], 0)
    prev_inline = (nv > 0) & (prev_nv == tile)

    @pl.when((prev_nv > 0) & jnp.logical_not(prev_inline))
    def _():
        rows_out(i - 1, 1 - slot, prev_nv, False)

    @pl.when((nv == 0) & (prev_nv > 0))
    def _():
        wait_gather()

    def expert_block(scatter_prev):
        wait_gather()
        for r in range(tile):
            row_in(src_ref[(i + 1) * tile + r], r, 1 - slot).start()
        if scatter_prev:
            for r in range(tile):
                row_out(src_ref[(i - 1) * tile + r], r, 1 - slot).start()

        x = jnp.concatenate([xbuf[slot, pl.ds(j, tile, stride=X2_ROWS), :] for j in range(D_MODEL // LANES)], axis=1)
        cw = xbuf[slot, pl.ds(D_MODEL // LANES, tile, stride=X2_ROWS), :]
        ms = jnp.mean(x * x, axis=-1, keepdims=True)
        hn = (x * lax.rsqrt(ms + EPS) * nf_ref[...]).astype(BF16)
        moe = None
        for c, (wg, wu, wd) in enumerate(((wgl_ref, wul_ref, wdl_ref), (wgh_ref, wuh_ref, wdh_ref))):
            gate = _dot(hn, wg[0])
            up = _dot(hn, wu[0])
            act = (gate * jax.nn.sigmoid(gate) * up).astype(BF16)
            term = cw[:, c:c + 1] * _dot(act, wd[0])
            moe = term if moe is None else moe + term
        obuf[slot] = x + moe

    @pl.when(prev_inline)
    def _():
        expert_block(True)

    @pl.when((nv > 0) & (prev_nv != tile))
    def _():
        expert_block(False)

    @pl.when((i == n_tiles - 1) & (i >= 1))
    def _():
        drain_out(i - 1, 1 - slot)


def _moe(x2e, norm_ffn, wg, wu, wd, tile_elo, tile_ehi, tile_nv, src, tile=MOE_TILE):
    n = x2e.shape[0] // X2_ROWS
    n_tiles = tile_nv.shape[0]
    w_in = lambda sel: pl.BlockSpec((1, D_MODEL, D_EXPERT), lambda i, elo, ehi, nv, s: ((elo, ehi)[sel][i], 0, 0))
    w_dn = lambda sel: pl.BlockSpec((1, D_EXPERT, D_MODEL), lambda i, elo, ehi, nv, s: ((elo, ehi)[sel][i], 0, 0))
    grid_spec = pltpu.PrefetchScalarGridSpec(
        num_scalar_prefetch=4,
        grid=(n_tiles,),
        in_specs=[
            pl.BlockSpec(memory_space=pl.ANY),
            pl.BlockSpec((1, D_MODEL), lambda i, *_: (0, 0)),
            w_in(0), w_in(0), w_dn(0), w_in(1), w_in(1), w_dn(1),
        ],
        out_specs=pl.BlockSpec(memory_space=pl.ANY),
        scratch_shapes=[
            pltpu.VMEM((2, tile * X2_ROWS, LANES), F32),
            pltpu.VMEM((2, tile, D_MODEL), F32),
            pltpu.SemaphoreType.DMA((2,)),
            pltpu.SemaphoreType.DMA((2,)),
        ],
    )
    return pl.pallas_call(
        functools.partial(_moe_kernel, tile=tile, n_tiles=n_tiles),
        out_shape=jax.ShapeDtypeStruct((n, D_MODEL), F32),
        grid_spec=grid_spec,
        compiler_params=_cparams(("arbitrary",)),
        name="moe",
    )(tile_elo, tile_ehi, tile_nv, src, x2e, norm_ffn, wg, wu, wd, wg, wu, wd)


def _moe_plan(cls, n, tile):
    n_tiles = (n + N_CLASSES * (tile - 1)) // tile + 1
    order = jnp.argsort(cls, stable=True).astype(jnp.int32)
    counts = jnp.sum((cls[:, None] == jnp.arange(N_CLASSES, dtype=jnp.int32)[None, :]).astype(jnp.int32), axis=0)
    tiles_per = (counts + tile - 1) // tile
    tile_end = jnp.cumsum(tiles_per)
    tile_start = tile_end - tiles_per
    tok_start = jnp.cumsum(counts) - counts
    t = jnp.arange(n_tiles, dtype=jnp.int32)
    tile_cls = jnp.minimum(jnp.sum((t[:, None] >= tile_end[None, :]).astype(jnp.int32), axis=1), N_CLASSES - 1)
    rank0 = (t - tile_start[tile_cls]) * tile
    tile_nv = jnp.clip(counts[tile_cls] - rank0, 0, tile).astype(jnp.int32)
    tile_nv = jnp.where(t < tile_end[-1], tile_nv, 0)
    r = jnp.arange(tile, dtype=jnp.int32)
    pos = tok_start[tile_cls][:, None] + rank0[:, None] + r[None, :]
    valid = r[None, :] < tile_nv[:, None]
    src = jnp.where(valid, order[jnp.clip(pos, 0, n - 1)], 0).reshape(-1).astype(jnp.int32)
    pair_lo = jnp.array([0, 0, 0, 1, 1, 2], jnp.int32)
    pair_hi = jnp.array([1, 2, 3, 2, 3, 3], jnp.int32)
    grp = tile_cls // N_PAIRS
    tile_elo = (grp * EXPERTS_PER_GROUP + pair_lo[tile_cls % N_PAIRS]).astype(jnp.int32)
    tile_ehi = (grp * EXPERTS_PER_GROUP + pair_hi[tile_cls % N_PAIRS]).astype(jnp.int32)
    return tile_elo, tile_ehi, tile_nv, src


def kernel(x, mem, positions, norm_mix, norm_mem, w_in, b_gate, qn_dil, kn_dil, qn_mem, kn_mem, w_mem_kv,
           w_o_dil, w_o_sb, w_o_mem, w_out, norm_ffn, w_router_group, b_router_group, w_router_expert,
           b_router_expert, w_exp_gate, w_exp_up, w_exp_down):
    b, s, d = x.shape
    n = b * s
    assert d == D_MODEL and w_in.shape == (1, D_MODEL, IN_COLS) and s % (BLOCK_Q * DILATIONS[-1]) == 0
    off_g = IN_COLS - GATE_COLS
    x2d = x.reshape(n, d)

    w_in_p = jnp.concatenate([w_in[0][:, off_g:], w_in[0][:, :off_g]], axis=1).astype(BF16)
    proj = _in_proj(x2d, norm_mix, w_in_p)
    proj3 = proj.reshape(b, s, IN_COLS)

    inv_freq = ROPE_THETA ** (-jnp.arange(0, HEAD_DIM, 2, dtype=F32) / HEAD_DIM)
    invf = jnp.tile(inv_freq, LANES // (HEAD_DIM // 2))[None, :]
    sign = jnp.tile(jnp.concatenate([-jnp.ones(HEAD_DIM // 2, F32), jnp.ones(HEAD_DIM // 2, F32)]), 2)[None, :]
    pos_b = jnp.broadcast_to(positions.astype(F32).reshape(n, 1), (n, LANES))
    cos_t, sin_t = _rope_tables(pos_b, invf, sign)
    o_dil = _dilated(proj3, cos_t.reshape(b, s, LANES), sin_t.reshape(b, s, LANES),
                     jnp.tile(qn_dil, (1, 2)), jnp.tile(kn_dil, (1, 2)))

    o_sb = _stick_breaking(proj3)

    k_mem, v_mem = _mem_kv(mem, norm_mem, w_mem_kv[0].astype(BF16), kn_mem)
    o_mem = _mem_attn(proj3, k_mem, v_mem, qn_mem)

    pad = LANES - N_GROUPS - N_EXPERTS
    w_rt = jnp.concatenate([w_router_group[0], w_router_expert[0], jnp.zeros((D_MODEL, pad), F32)], axis=1)
    b_rt = jnp.concatenate([b_router_group[0], b_router_expert[0], jnp.zeros((pad,), F32)])[:, None]
    x2e, route = _merge(x2d, proj, b_gate, o_dil.reshape(n, DIL_WIDTH), o_sb.reshape(n, SB_WIDTH),
                        o_mem.reshape(n, MEM_WIDTH), w_o_dil[0].astype(BF16), w_o_sb[0].astype(BF16),
                        w_o_mem[0].astype(BF16), w_out[0].astype(BF16), norm_ffn, w_rt, b_rt)

    tile_elo, tile_ehi, tile_nv, src = _moe_plan(route[2].astype(jnp.int32), n, MOE_TILE)
    out = _moe(x2e, norm_ffn, w_exp_gate[0].astype(BF16), w_exp_up[0].astype(BF16),
               w_exp_down[0].astype(BF16), tile_elo, tile_ehi, tile_nv, src)
    return out.reshape(b, s, d)
```

```python
import functools

import jax
import jax.numpy as jnp
from jax import lax
from jax.experimental import pallas as pl
from jax.experimental.pallas import tpu as pltpu

F32 = jnp.float32
BF16 = jnp.bfloat16

D_MODEL = 1024
HEAD_DIM = 64
LANES = 128
BLOCK_Q = 128
ROPE_THETA = 10000.0
EPS = 1e-6
DILATIONS = (1, 4, 16)
DIL_WIDTH = 512
SB_WIDTH = 512
MEM_WIDTH = 512
MEM_HEADS = 4
MEM_HEAD_DIM = 128
MEM_LEN = 256
GATE_COLS = 3 * D_MODEL
A_COLS = 9 * DIL_WIDTH
B_COLS = 3 * SB_WIDTH
IN_COLS = A_COLS + B_COLS + MEM_WIDTH + GATE_COLS
CB_DIL = GATE_COLS // LANES
CB_SB = CB_DIL + A_COLS // LANES
CB_MEM = CB_SB + B_COLS // LANES
N_GROUPS = 4
EXPERTS_PER_GROUP = 4
N_EXPERTS = 16
D_EXPERT = 512
N_PAIRS = 6
N_CLASSES = N_GROUPS * N_PAIRS
MOE_TILE = 256
STAGE_GAP = 2
DMA_UNROLL = 8
X2_ROWS = 16
NEG = -1e30
VMEM_LIMIT = 56 * 1024 * 1024


def _cparams(sem):
    return pltpu.CompilerParams(dimension_semantics=sem, vmem_limit_bytes=VMEM_LIMIT)


def _dot(a, b):
    return jnp.dot(a, b, preferred_element_type=F32)


def _dot_nt(a, b):
    return lax.dot_general(a, b, (((1,), (1,)), ((), ())), preferred_element_type=F32)


def _split_bf16(a):
    hi = a.astype(BF16)
    return hi, (a - hi.astype(F32)).astype(BF16)


def _head_masks(rows):
    lane = lax.broadcasted_iota(jnp.int32, (rows, LANES), 1)
    return lane < HEAD_DIM


def _inproj_kernel(x_ref, g_ref, w_ref, o_ref, *, chunk):
    x = x_ref[...]
    ms = jnp.mean(x * x, axis=-1, keepdims=True)
    h = (x * lax.rsqrt(ms + EPS) * g_ref[...]).astype(BF16)
    for c in range(IN_COLS // chunk):
        sl = slice(c * chunk, (c + 1) * chunk)
        o_ref[:, sl] = _dot(h, w_ref[:, sl]).astype(BF16)


def _in_proj(x2d, gain, w_bf16, tm=256, chunk=512):
    n = x2d.shape[0]
    return pl.pallas_call(
        functools.partial(_inproj_kernel, chunk=chunk),
        out_shape=jax.ShapeDtypeStruct((n, IN_COLS), BF16),
        grid=(n // tm,),
        in_specs=[
            pl.BlockSpec((tm, D_MODEL), lambda i: (i, 0)),
            pl.BlockSpec((1, D_MODEL), lambda i: (0, 0)),
            pl.BlockSpec((D_MODEL, IN_COLS), lambda i: (0, 0), pipeline_mode=pl.Buffered(1)),
        ],
        out_specs=pl.BlockSpec((tm, IN_COLS), lambda i: (i, 0)),
        compiler_params=_cparams(("parallel",)),
        name="in_proj",
    )(x2d, gain, w_bf16)


def _rope_kernel(pos_ref, invf_ref, sign_ref, cos_ref, sin_ref):
    ang = pos_ref[...] * invf_ref[...]
    cos_ref[...] = jnp.cos(ang)
    sin_ref[...] = jnp.sin(ang) * sign_ref[...]


def _rope_tables(pos_b, invf, sign, tm=2048):
    n = pos_b.shape[0]
    row = pl.BlockSpec((tm, LANES), lambda i: (i, 0))
    vec = pl.BlockSpec((1, LANES), lambda i: (0, 0))
    return pl.pallas_call(
        _rope_kernel,
        out_shape=(jax.ShapeDtypeStruct((n, LANES), F32),) * 2,
        grid=(n // tm,),
        in_specs=[row, vec, vec],
        out_specs=(row, row),
        compiler_params=_cparams(("parallel",)),
        name="rope_tab",
    )(pos_b, invf, sign)


def _dil_kernel(q0, k0, v0, q1, k1, v1, q2, k2, v2, cos_ref, sin_ref, qn_ref, kn_ref, o_ref,
                qp, km0, km1, vm0, vm1, st, stq, stk, stv, stg, *, seq):
    qkv = ((q0, k0, v0), (q1, k1, v1), (q2, k2, v2))
    chunk = 2 * BLOCK_Q
    n_chunks = seq // chunk

    gi = lax.broadcasted_iota(jnp.int32, (LANES, LANES), 0) // HEAD_DIM
    gj = lax.broadcasted_iota(jnp.int32, (LANES, LANES), 1) // HEAD_DIM
    head_mean = jnp.where(gi == gj, 1.0 / HEAD_DIM, 0.0).astype(BF16)
    lane_c = lax.broadcasted_iota(jnp.int32, (chunk, LANES), 1)
    first_half = (lane_c % HEAD_DIM) < (HEAD_DIM // 2)

    def norm_rope(t_ref, gain, rows, scale):
        t = t_ref[0, rows, :].astype(F32)
        hi, lo = _split_bf16(t * t)
        tn = t * lax.rsqrt(_dot(hi, head_mean) + _dot(lo, head_mean) + EPS) * gain
        rot = jnp.where(first_half, pltpu.roll(tn, LANES - HEAD_DIM // 2, 1), pltpu.roll(tn, HEAD_DIM // 2, 1))
        out = tn * cos_ref[0, rows, :] + rot * sin_ref[0, rows, :]
        return out * scale if scale != 1.0 else out

    def put(g, dst, qb, kb, vb):
        h = _head_masks(kb.shape[0])
        zero = jnp.zeros_like(kb)
        qp[g, dst, :] = qb
        km0[g, dst, :] = jnp.where(h, kb, zero)
        km1[g, dst, :] = jnp.where(h, zero, kb)
        vm0[g, dst, :] = jnp.where(h, vb, zero)
        vm1[g, dst, :] = jnp.where(h, zero, vb)

    def prep(i, _):
        rows = pl.ds(pl.multiple_of(i * chunk, chunk), chunk)
        for g, dil in enumerate(DILATIONS):
            q_ref, k_ref, v_ref = qkv[g]
            qn = norm_rope(q_ref, qn_ref[...], rows, HEAD_DIM ** -0.5)
            kn = norm_rope(k_ref, kn_ref[...], rows, 1.0)
            if dil == 1:
                put(g, rows, qn.astype(BF16), kn.astype(BF16), v_ref[0, rows, :])
                continue
            stq[g - 1] = qn
            stk[g - 1] = kn
            stv[g - 1] = v_ref[0, rows, :].astype(F32)
            piece, sub = chunk // dil, seq // dil
            for c in range(dil):
                src = pl.ds(c, piece, stride=dil)
                dst = pl.ds(pl.multiple_of(c * sub + i * piece, piece), piece)
                put(g, dst, stq[g - 1, src, :].astype(BF16), stk[g - 1, src, :].astype(BF16),
                    stv[g - 1, src, :].astype(BF16))
        return 0

    lax.fori_loop(0, n_chunks, prep, 0, unroll=2)

    qi = lax.broadcasted_iota(jnp.int32, (BLOCK_Q, BLOCK_Q), 0)
    kj = lax.broadcasted_iota(jnp.int32, (BLOCK_Q, BLOCK_Q), 1)
    bias_cur = jnp.where(kj <= qi, 0.0, NEG).astype(F32)
    bias_prev = jnp.where(kj >= qi, 0.0, NEG).astype(F32)
    bias = {BLOCK_Q: jnp.concatenate([bias_cur] * 2, axis=1),
            2 * BLOCK_Q: jnp.concatenate([bias_prev, bias_cur] * 2, axis=1)}
    hq = _head_masks(BLOCK_Q)
    ones = {nk: jnp.where(_head_masks(nk), 1.0, 0.0).astype(BF16) for nk in bias}

    def scores(g, q_lo, k_lo, nk):
        keys = jnp.concatenate([km0[g, k_lo:k_lo + nk, :], km1[g, k_lo:k_lo + nk, :]], axis=0)
        return _dot_nt(qp[g, q_lo:q_lo + BLOCK_Q, :], keys)

    def softmax(s, nk):
        s = s + bias[nk]
        mx = [jnp.max(s[:, h * nk:(h + 1) * nk], axis=-1, keepdims=True) for h in range(2)]
        m_all = jnp.concatenate([jnp.broadcast_to(m, (BLOCK_Q, nk)) for m in mx], axis=1)
        return jnp.exp(s - m_all).astype(BF16), jnp.where(hq, mx[0], mx[1])

    def weighted(p, mt, g, q_lo, k_lo, nk):
        rhs = jnp.concatenate([
            jnp.concatenate([vm0[g, k_lo:k_lo + nk, :], ones[nk]], axis=1),
            jnp.concatenate([vm1[g, k_lo:k_lo + nk, :], 1.0 - ones[nk]], axis=1)], axis=0)
        pv = _dot(p, rhs)
        st[g, 0, q_lo:q_lo + BLOCK_Q, :] = pv[:, :LANES]
        st[g, 1, q_lo:q_lo + BLOCK_Q, :] = pv[:, LANES:]
        st[g, 2, q_lo:q_lo + BLOCK_Q, :] = mt

    blocks = []
    for g, dil in enumerate(DILATIONS):
        sub = seq // dil
        for c in range(dil):
            for n in range(sub // BLOCK_Q):
                q_lo = c * sub + n * BLOCK_Q
                blocks.append((g, q_lo, q_lo - BLOCK_Q, 2 * BLOCK_Q) if n else (g, q_lo, q_lo, BLOCK_Q))
    ss, ps = {}, {}
    for i in range(len(blocks) + 2 * STAGE_GAP):
        if i >= 2 * STAGE_GAP:
            j = i - 2 * STAGE_GAP
            weighted(*ps.pop(j), *blocks[j])
        if STAGE_GAP <= i < len(blocks) + STAGE_GAP:
            j = i - STAGE_GAP
            ps[j] = softmax(ss.pop(j), blocks[j][3])
        if i < len(blocks):
            ss[i] = scores(*blocks[i])

    def merge(i, _):
        rows = pl.ds(pl.multiple_of(i * chunk, chunk), chunk)
        parts = []
        for g, dil in enumerate(DILATIONS):
            if dil == 1:
                parts.append([st[g, k, rows, :] for k in range(3)])
                continue
            piece, sub = chunk // dil, seq // dil
            for c in range(dil):
                src = pl.ds(pl.multiple_of(c * sub + i * piece, piece), piece)
                dst = pl.ds(c, piece, stride=dil)
                for k in range(3):
                    stg[g - 1, k, dst, :] = st[g, k, src, :]
            parts.append([stg[g - 1, k] for k in range(3)])
        m_max = functools.reduce(jnp.maximum, [m for _, _, m in parts])
        num = den = None
        for pv, rowsum, m in parts:
            w = jnp.exp(m - m_max)
            num = w * pv if num is None else num + w * pv
            den = w * rowsum if den is None else den + w * rowsum
        o_ref[0, rows, :] = (num / den).astype(BF16)
        return 0

    lax.fori_loop(0, n_chunks, merge, 0)


def _dilated(proj3, cos3, sin3, qn2, kn2):
    b, s, _ = proj3.shape
    specs = []
    for g in range(3):
        for t in range(3):
            cb = CB_DIL + (t * 3 + g) * 4
            specs.append(pl.BlockSpec((1, s, LANES), lambda bi, j, cb=cb: (bi, 0, cb + j)))
    tab = pl.BlockSpec((1, s, LANES), lambda bi, j: (bi, 0, 0))
    vec = pl.BlockSpec((1, LANES), lambda bi, j: (0, 0))
    chunk = 2 * BLOCK_Q
    return pl.pallas_call(
        functools.partial(_dil_kernel, seq=s),
        out_shape=jax.ShapeDtypeStruct((b, s, DIL_WIDTH), BF16),
        grid=(b, DIL_WIDTH // LANES),
        in_specs=specs + [tab, tab, vec, vec],
        out_specs=pl.BlockSpec((1, s, LANES), lambda bi, j: (bi, 0, j)),
        scratch_shapes=[pltpu.VMEM((3, s, LANES), BF16)] * 5
        + [pltpu.VMEM((3, 3, s, LANES), F32)]
        + [pltpu.VMEM((2, chunk, LANES), F32)] * 3
        + [pltpu.VMEM((2, 3, chunk, LANES), F32)],
        compiler_params=_cparams(("parallel", "arbitrary")),
        name="dilated",
    )(*([proj3] * 9), cos3, sin3, qn2, kn2)


def _sb_kernel(q_ref, k_ref, v_ref, o_ref, qs, ks, vs, *, seq):
    nb = seq // BLOCK_Q
    qi = lax.broadcasted_iota(jnp.int32, (BLOCK_Q, 2 * BLOCK_Q), 0)
    kj = lax.broadcasted_iota(jnp.int32, (BLOCK_Q, 2 * BLOCK_Q), 1) % BLOCK_Q
    causal = kj < qi
    uj = lax.broadcasted_iota(jnp.int32, (2 * BLOCK_Q, 2 * BLOCK_Q), 0) % BLOCK_Q
    us = lax.broadcasted_iota(jnp.int32, (2 * BLOCK_Q, 2 * BLOCK_Q), 1)
    suffix = jnp.where((uj > us) | (us >= BLOCK_Q), 1.0, 0.0).astype(BF16)
    h0 = _head_masks(BLOCK_Q)

    def prep(i, _):
        rows = pl.ds(pl.multiple_of(i * BLOCK_Q, BLOCK_Q), BLOCK_Q)
        qs[rows, :] = q_ref[0, rows, :] * jnp.asarray(HEAD_DIM ** -0.5, BF16)
        k = k_ref[0, rows, :]
        v = v_ref[0, rows, :]
        zero = jnp.zeros_like(k)
        ks[i, :BLOCK_Q, :] = jnp.where(h0, k, zero)
        ks[i, BLOCK_Q:, :] = jnp.where(h0, zero, k)
        vs[i, :BLOCK_Q, :] = jnp.where(h0, v, zero)
        vs[i, BLOCK_Q:, :] = jnp.where(h0, zero, v)
        return 0

    lax.fori_loop(0, nb, prep, 0)

    def scores(rb, kb):
        return _dot_nt(qs[pl.ds(rb * BLOCK_Q, BLOCK_Q), :], ks[kb])

    def log_weights(z, diag):
        neg = jnp.minimum(z, 0.0)
        sp = jnp.log(1.0 + jnp.exp(-jnp.abs(z)))
        log_beta = neg - sp
        log_1m = (neg - z) - sp
        if diag:
            log_1m = jnp.where(causal, log_1m, 0.0)
        hi, lo = _split_bf16(log_1m)
        lhs = jnp.concatenate([jnp.concatenate([hi[:, sl], lo[:, sl]], axis=1)
                               for sl in (slice(0, BLOCK_Q), slice(BLOCK_Q, 2 * BLOCK_Q))], axis=0)
        su = _dot(lhs, suffix)
        w = log_beta + jnp.concatenate([su[:BLOCK_Q, :BLOCK_Q], su[BLOCK_Q:, :BLOCK_Q]], axis=1)
        tot = jnp.concatenate([su[:BLOCK_Q, BLOCK_Q:], su[BLOCK_Q:, BLOCK_Q:]], axis=1)
        return w, tot

    def accumulate(state, w, tot, rb, kb):
        diag = kb == rb
        c, acc = (None, None) if diag else state
        a = jnp.exp(w if diag else w + c)
        if diag:
            a = jnp.where(causal, a, 0.0)
        pv = _dot(a.astype(BF16), vs[kb])
        acc = pv if diag else acc + pv
        if kb == 0:
            o_ref[0, pl.ds(rb * BLOCK_Q, BLOCK_Q), :] = acc.astype(BF16)
        elif diag:
            c = tot
        else:
            c = c + tot
        return c, acc

    pairs = [(rb, kb) for rb in range(nb) for kb in range(rb, -1, -1)]
    zs, wts, state = {}, {}, None
    for i in range(len(pairs) + 2 * STAGE_GAP):
        if i >= 2 * STAGE_GAP:
            j = i - 2 * STAGE_GAP
            state = accumulate(state, *wts.pop(j), *pairs[j])
        if STAGE_GAP <= i < len(pairs) + STAGE_GAP:
            j = i - STAGE_GAP
            wts[j] = log_weights(zs.pop(j), pairs[j][0] == pairs[j][1])
        if i < len(pairs):
            zs[i] = scores(*pairs[i])


def _stick_breaking(proj3):
    b, s, _ = proj3.shape
    nb = s // BLOCK_Q
    specs = [pl.BlockSpec((1, s, LANES), lambda bi, j, cb=CB_SB + t * 4: (bi, 0, cb + j)) for t in range(3)]
    return pl.pallas_call(
        functools.partial(_sb_kernel, seq=s),
        out_shape=jax.ShapeDtypeStruct((b, s, SB_WIDTH), BF16),
        grid=(b, SB_WIDTH // LANES),
        in_specs=specs,
        out_specs=pl.BlockSpec((1, s, LANES), lambda bi, j: (bi, 0, j)),
        scratch_shapes=[pltpu.VMEM((s, LANES), BF16)] + [pltpu.VMEM((nb, 2 * BLOCK_Q, LANES), BF16)] * 2,
        compiler_params=_cparams(("parallel", "arbitrary")),
        name="stickbrk",
    )(proj3, proj3, proj3)


def _memkv_kernel(mem_ref, g_ref, w_ref, kn_ref, k_ref, v_ref):
    x = mem_ref[0]
    ms = jnp.mean(x * x, axis=-1, keepdims=True)
    h = (x * lax.rsqrt(ms + EPS) * g_ref[...]).astype(BF16)
    kv = _dot(h, w_ref[...])
    for hd in range(MEM_HEADS):
        sl = slice(hd * MEM_HEAD_DIM, (hd + 1) * MEM_HEAD_DIM)
        kh = kv[:, sl]
        msk = jnp.mean(kh * kh, axis=-1, keepdims=True)
        k_ref[0, :, sl] = (kh * lax.rsqrt(msk + EPS) * kn_ref[...]).astype(BF16)
    v_ref[0] = kv[:, MEM_WIDTH:].astype(BF16)


def _mem_kv(mem, gain, w_bf16, kn):
    b = mem.shape[0]
    out = jax.ShapeDtypeStruct((b, MEM_LEN, MEM_WIDTH), BF16)
    blk = pl.BlockSpec((1, MEM_LEN, MEM_WIDTH), lambda i: (i, 0, 0))
    return pl.pallas_call(
        _memkv_kernel,
        out_shape=(out, out),
        grid=(b,),
        in_specs=[
            pl.BlockSpec((1, MEM_LEN, D_MODEL), lambda i: (i, 0, 0)),
            pl.BlockSpec((1, D_MODEL), lambda i: (0, 0)),
            pl.BlockSpec((D_MODEL, 2 * MEM_WIDTH), lambda i: (0, 0)),
            pl.BlockSpec((1, MEM_HEAD_DIM), lambda i: (0, 0)),
        ],
        out_specs=(blk, blk),
        compiler_params=_cparams(("parallel",)),
        name="mem_kv",
    )(mem, gain, w_bf16, kn)


def _memattn_kernel(q_ref, k_ref, v_ref, qn_ref, o_ref):
    q = q_ref[0].astype(F32)
    ms = jnp.mean(q * q, axis=-1, keepdims=True)
    qn = (q * lax.rsqrt(ms + EPS) * qn_ref[...]).astype(BF16)
    s = _dot_nt(qn, k_ref[0]) * (MEM_HEAD_DIM ** -0.5)
    m = jnp.max(s, axis=-1, keepdims=True)
    p = jnp.exp(s - m)
    den = jnp.sum(p, axis=-1, keepdims=True)
    o_ref[0] = (_dot(p.astype(BF16), v_ref[0]) / den).astype(BF16)


def _mem_attn(proj3, k, v, qn, tq=1024):
    b, s, _ = proj3.shape
    kv = pl.BlockSpec((1, MEM_LEN, MEM_HEAD_DIM), lambda bi, h, i: (bi, 0, h))
    return pl.pallas_call(
        _memattn_kernel,
        out_shape=jax.ShapeDtypeStruct((b, s, MEM_WIDTH), BF16),
        grid=(b, MEM_HEADS, s // tq),
        in_specs=[
            pl.BlockSpec((1, tq, MEM_HEAD_DIM), lambda bi, h, i: (bi, i, CB_MEM + h)),
            kv, kv,
            pl.BlockSpec((1, MEM_HEAD_DIM), lambda bi, h, i: (0, 0)),
        ],
        out_specs=pl.BlockSpec((1, tq, MEM_HEAD_DIM), lambda bi, h, i: (bi, i, h)),
        compiler_params=_cparams(("parallel", "parallel", "arbitrary")),
        name="mem_attn",
    )(proj3, k, v, qn)


def _merge_kernel(x_ref, gl_ref, bg_ref, od_ref, os_ref, om_ref, wd_ref, ws_ref, wm_ref, wo_ref,
                  nf_ref, wr_ref, br_ref, x2_ref, r_ref):
    merged = None
    for i, (o_ref, w_ref) in enumerate(((od_ref, wd_ref), (os_ref, ws_ref), (om_ref, wm_ref))):
        sl = slice(i * D_MODEL, (i + 1) * D_MODEL)
        gate = jax.nn.sigmoid(gl_ref[:, sl].astype(F32) + bg_ref[:, sl])
        term = gate * _dot(o_ref[...], w_ref[...])
        merged = term if merged is None else merged + term
    x2 = x_ref[...] + _dot(merged.astype(BF16), wo_ref[...])
    tm = x2.shape[0]
    for j in range(D_MODEL // LANES):
        x2_ref[pl.ds(j, tm, stride=X2_ROWS), :] = x2[:, j * LANES:(j + 1) * LANES]

    ms = jnp.mean(x2 * x2, axis=-1, keepdims=True)
    hn = x2 * lax.rsqrt(ms + EPS) * nf_ref[...]
    h_hi, h_lo = _split_bf16(hn)
    w_hi, w_lo = _split_bf16(wr_ref[...])
    lt = (_dot(h_hi, w_hi) + _dot(h_lo, w_hi) + _dot(h_hi, w_lo)).T + br_ref[...]
    row = lambda i: lt[i:i + 1, :]
    first_max = lambda v, mx: jnp.where(v[0] == mx, 0, jnp.where(v[1] == mx, 1, jnp.where(v[2] == mx, 2, 3)))
    gl = [row(i) for i in range(N_GROUPS)]
    gmax = functools.reduce(jnp.maximum, gl)
    g_top = 1.0 / functools.reduce(lambda a, c: a + c, [jnp.exp(v - gmax) for v in gl])
    g_idx = first_max(gl, gmax)
    el = []
    for e in range(EXPERTS_PER_GROUP):
        v = row(N_GROUPS + 3 * EXPERTS_PER_GROUP + e)
        for g in range(N_GROUPS - 2, -1, -1):
            v = jnp.where(g_idx == g, row(N_GROUPS + g * EXPERTS_PER_GROUP + e), v)
        el.append(v)
    emax = functools.reduce(jnp.maximum, el)
    a_idx = first_max(el, emax)
    rest = [jnp.where(a_idx == e, -jnp.inf, el[e]) for e in range(EXPERTS_PER_GROUP)]
    rmax = functools.reduce(jnp.maximum, rest)
    b_idx = first_max(rest, rmax)
    ratio = jnp.exp(rmax - emax)
    w_a = g_top / (1.0 + ratio)
    w_b = g_top * ratio / (1.0 + ratio)
    lo = jnp.minimum(a_idx, b_idx)
    hi = jnp.maximum(a_idx, b_idx)
    pair = jnp.where(lo == 0, hi - 1, jnp.where(lo == 1, hi + 1, 5))
    cls = g_idx * N_PAIRS + pair
    a_first = a_idx < b_idx
    tm = lt.shape[1]
    ri = lax.broadcasted_iota(jnp.int32, (8, tm), 0)
    info = jnp.where(ri == 0, jnp.where(a_first, w_a, w_b),
                     jnp.where(ri == 1, jnp.where(a_first, w_b, w_a), jnp.where(ri == 2, cls.astype(F32), 0.0)))
    r_ref[...] = info
    x2_ref[pl.ds(D_MODEL // LANES, tm, stride=X2_ROWS), :] = jnp.concatenate(
        [info, jnp.zeros((LANES - 8, tm), F32)], axis=0).T
    for j in range(D_MODEL // LANES + 1, X2_ROWS):
        x2_ref[pl.ds(j, tm, stride=X2_ROWS), :] = jnp.zeros((tm, LANES), F32)


def _merge(x2d, proj, b_gate, o_dil, o_sb, o_mem, w_o_dil, w_o_sb, w_o_mem, w_out, norm_ffn, w_rt, b_rt, tm=512):
    n = x2d.shape[0]
    row = lambda w: pl.BlockSpec((tm, w), lambda i: (i, 0))
    full = lambda a, bdim: pl.BlockSpec((a, bdim), lambda i: (0, 0))
    return pl.pallas_call(
        _merge_kernel,
        out_shape=(jax.ShapeDtypeStruct((n * X2_ROWS, LANES), F32), jax.ShapeDtypeStruct((8, n), F32)),
        grid=(n // tm,),
        in_specs=[
            row(D_MODEL), row(GATE_COLS), full(1, GATE_COLS),
            row(DIL_WIDTH), row(SB_WIDTH), row(MEM_WIDTH),
            full(DIL_WIDTH, D_MODEL), full(SB_WIDTH, D_MODEL), full(MEM_WIDTH, D_MODEL), full(D_MODEL, D_MODEL),
            full(1, D_MODEL), full(D_MODEL, LANES), full(LANES, 1),
        ],
        out_specs=(pl.BlockSpec((tm * X2_ROWS, LANES), lambda i: (i, 0)), pl.BlockSpec((8, tm), lambda i: (0, i))),
        compiler_params=_cparams(("parallel",)),
        name="merge",
    )(x2d, proj, b_gate, o_dil, o_sb, o_mem, w_o_dil, w_o_sb, w_o_mem, w_out, norm_ffn, w_rt, b_rt)


def _moe_kernel(elo_ref, ehi_ref, nv_ref, src_ref,
                x_hbm, nf_ref, wgl_ref, wul_ref, wdl_ref, wgh_ref, wuh_ref, wdh_ref,
                out_hbm, xbuf, obuf, gsem, ssem, *, tile, n_tiles):
    i = pl.program_id(0)
    slot = i % 2

    def row_in(tok, r, s):
        return pltpu.make_async_copy(x_hbm.at[pl.ds(pl.multiple_of(tok * X2_ROWS, X2_ROWS), X2_ROWS), :],
                                     xbuf.at[s, pl.ds(r * X2_ROWS, X2_ROWS), :], gsem.at[s])

    def row_out(tok, r, s):
        return pltpu.make_async_copy(obuf.at[s, pl.ds(r, 1), :], out_hbm.at[pl.ds(tok, 1), :], ssem.at[s])

    def rows_out(t, s, count, wait):
        def one(r):
            if wait:
                row_out(0, r, s).wait()
            else:
                row_out(src_ref[t * tile + r], r, s).start()

        def chunk(c, _):
            for u in range(DMA_UNROLL):
                one(c * DMA_UNROLL + u)
            return 0

        def single(r, _):
            one(r)
            return 0

        full = count // DMA_UNROLL
        lax.fori_loop(0, full, chunk, 0)
        lax.fori_loop(full * DMA_UNROLL, count, single, 0)

    def drain_out(t, s):
        rows_out(t, s, nv_ref[t], True)

    def wait_gather():
        for r in range(tile):
            row_in(0, r, slot).wait()

    nv = nv_ref[i]

    @pl.when((i == 0) & (nv > 0))
    def _():
        def chunk(c, _):
            for u in range(DMA_UNROLL):
                r = c * DMA_UNROLL + u
                row_in(src_ref[r], r, 0).start()
            return 0
        lax.fori_loop(0, tile // DMA_UNROLL, chunk, 0)

    @pl.when(i >= 2)
    def _():
        drain_out(i - 2, slot)

    prev_nv = jnp.where(i > 0, nv_ref[jnp.maximum(i - 1, 0)], 0)
    prev_inline = (nv > 0) & (prev_nv == tile)

    @pl.when((prev_nv > 0) & jnp.logical_not(prev_inline))
    def _():
        rows_out(i - 1, 1 - slot, prev_nv, False)

    @pl.when((nv == 0) & (prev_nv > 0))
    def _():
        wait_gather()

    def expert_block(scatter_prev):
        wait_gather()
        for r in range(tile):
            row_in(src_ref[(i + 1) * tile + r], r, 1 - slot).start()
        if scatter_prev:
            for r in range(tile):
                row_out(src_ref[(i - 1) * tile + r], r, 1 - slot).start()

        x = jnp.concatenate([xbuf[slot, pl.ds(j, tile, stride=X2_ROWS), :] for j in range(D_MODEL // LANES)], axis=1)
        cw = xbuf[slot, pl.ds(D_MODEL // LANES, tile, stride=X2_ROWS), :]
        ms = jnp.mean(x * x, axis=-1, keepdims=True)
        hn = (x * lax.rsqrt(ms + EPS) * nf_ref[...]).astype(BF16)
        moe = None
        for c, (wg, wu, wd) in enumerate(((wgl_ref, wul_ref, wdl_ref), (wgh_ref, wuh_ref, wdh_ref))):
            gate = _dot(hn, wg[0])
            up = _dot(hn, wu[0])
            act = (gate * jax.nn.sigmoid(gate) * up).astype(BF16)
            term = cw[:, c:c + 1] * _dot(act, wd[0])
            moe = term if moe is None else moe + term
        obuf[slot] = x + moe

    @pl.when(prev_inline)
    def _():
        expert_block(True)

    @pl.when((nv > 0) & (prev_nv != tile))
    def _():
        expert_block(False)

    @pl.when((i == n_tiles - 1) & (i >= 1))
    def _():
        drain_out(i - 1, 1 - slot)


def _moe(x2e, norm_ffn, wg, wu, wd, tile_elo, tile_ehi, tile_nv, src, tile=MOE_TILE):
    n = x2e.shape[0] // X2_ROWS
    n_tiles = tile_nv.shape[0]
    w_in = lambda sel: pl.BlockSpec((1, D_MODEL, D_EXPERT), lambda i, elo, ehi, nv, s: ((elo, ehi)[sel][i], 0, 0))
    w_dn = lambda sel: pl.BlockSpec((1, D_EXPERT, D_MODEL), lambda i, elo, ehi, nv, s: ((elo, ehi)[sel][i], 0, 0))
    grid_spec = pltpu.PrefetchScalarGridSpec(
        num_scalar_prefetch=4,
        grid=(n_tiles,),
        in_specs=[
            pl.BlockSpec(memory_space=pl.ANY),
            pl.BlockSpec((1, D_MODEL), lambda i, *_: (0, 0)),
            w_in(0), w_in(0), w_dn(0), w_in(1), w_in(1), w_dn(1),
        ],
        out_specs=pl.BlockSpec(memory_space=pl.ANY),
        scratch_shapes=[
            pltpu.VMEM((2, tile * X2_ROWS, LANES), F32),
            pltpu.VMEM((2, tile, D_MODEL), F32),
            pltpu.SemaphoreType.DMA((2,)),
            pltpu.SemaphoreType.DMA((2,)),
        ],
    )
    return pl.pallas_call(
        functools.partial(_moe_kernel, tile=tile, n_tiles=n_tiles),
        out_shape=jax.ShapeDtypeStruct((n, D_MODEL), F32),
        grid_spec=grid_spec,
        compiler_params=_cparams(("arbitrary",)),
        name="moe",
    )(tile_elo, tile_ehi, tile_nv, src, x2e, norm_ffn, wg, wu, wd, wg, wu, wd)


def _moe_plan(cls, n, tile):
    n_tiles = (n + N_CLASSES * (tile - 1)) // tile + 1
    order = jnp.argsort(cls, stable=True).astype(jnp.int32)
    counts = jnp.sum((cls[:, None] == jnp.arange(N_CLASSES, dtype=jnp.int32)[None, :]).astype(jnp.int32), axis=0)
    tiles_per = (counts + tile - 1) // tile
    tile_end = jnp.cumsum(tiles_per)
    tile_start = tile_end - tiles_per
    tok_start = jnp.cumsum(counts) - counts
    t = jnp.arange(n_tiles, dtype=jnp.int32)
    tile_cls = jnp.minimum(jnp.sum((t[:, None] >= tile_end[None, :]).astype(jnp.int32), axis=1), N_CLASSES - 1)
    rank0 = (t - tile_start[tile_cls]) * tile
    tile_nv = jnp.clip(counts[tile_cls] - rank0, 0, tile).astype(jnp.int32)
    tile_nv = jnp.where(t < tile_end[-1], tile_nv, 0)
    r = jnp.arange(tile, dtype=jnp.int32)
    pos = tok_start[tile_cls][:, None] + rank0[:, None] + r[None, :]
    valid = r[None, :] < tile_nv[:, None]
    src = jnp.where(valid, order[jnp.clip(pos, 0, n - 1)], 0).reshape(-1).astype(jnp.int32)
    pair_lo = jnp.array([0, 0, 0, 1, 1, 2], jnp.int32)
    pair_hi = jnp.array([1, 2, 3, 2, 3, 3], jnp.int32)
    grp = tile_cls // N_PAIRS
    tile_elo = (grp * EXPERTS_PER_GROUP + pair_lo[tile_cls % N_PAIRS]).astype(jnp.int32)
    tile_ehi = (grp * EXPERTS_PER_GROUP + pair_hi[tile_cls % N_PAIRS]).astype(jnp.int32)
    return tile_elo, tile_ehi, tile_nv, src


def kernel(x, mem, positions, norm_mix, norm_mem, w_in, b_gate, qn_dil, kn_dil, qn_mem, kn_mem, w_mem_kv,
           w_o_dil, w_o_sb, w_o_mem, w_out, norm_ffn, w_router_group, b_router_group, w_router_expert,
           b_router_expert, w_exp_gate, w_exp_up, w_exp_down):
    b, s, d = x.shape
    n = b * s
    assert d == D_MODEL and w_in.shape == (1, D_MODEL, IN_COLS) and s % (BLOCK_Q * DILATIONS[-1]) == 0
    off_g = IN_COLS - GATE_COLS
    x2d = x.reshape(n, d)

    w_in_p = jnp.concatenate([w_in[0][:, off_g:], w_in[0][:, :off_g]], axis=1).astype(BF16)
    proj = _in_proj(x2d, norm_mix, w_in_p)
    proj3 = proj.reshape(b, s, IN_COLS)

    inv_freq = ROPE_THETA ** (-jnp.arange(0, HEAD_DIM, 2, dtype=F32) / HEAD_DIM)
    invf = jnp.tile(inv_freq, LANES // (HEAD_DIM // 2))[None, :]
    sign = jnp.tile(jnp.concatenate([-jnp.ones(HEAD_DIM // 2, F32), jnp.ones(HEAD_DIM // 2, F32)]), 2)[None, :]
    pos_b = jnp.broadcast_to(positions.astype(F32).reshape(n, 1), (n, LANES))
    cos_t, sin_t = _rope_tables(pos_b, invf, sign)
    o_dil = _dilated(proj3, cos_t.reshape(b, s, LANES), sin_t.reshape(b, s, LANES),
                     jnp.tile(qn_dil, (1, 2)), jnp.tile(kn_dil, (1, 2)))

    o_sb = _stick_breaking(proj3)

    k_mem, v_mem = _mem_kv(mem, norm_mem, w_mem_kv[0].astype(BF16), kn_mem)
    o_mem = _mem_attn(proj3, k_mem, v_mem, qn_mem)

    pad = LANES - N_GROUPS - N_EXPERTS
    w_rt = jnp.concatenate([w_router_group[0], w_router_expert[0], jnp.zeros((D_MODEL, pad), F32)], axis=1)
    b_rt = jnp.concatenate([b_router_group[0], b_router_expert[0], jnp.zeros((pad,), F32)])[:, None]
    x2e, route = _merge(x2d, proj, b_gate, o_dil.reshape(n, DIL_WIDTH), o_sb.reshape(n, SB_WIDTH),
                        o_mem.reshape(n, MEM_WIDTH), w_o_dil[0].astype(BF16), w_o_sb[0].astype(BF16),
                        w_o_mem[0].astype(BF16), w_out[0].astype(BF16), norm_ffn, w_rt, b_rt)

    tile_elo, tile_ehi, tile_nv, src = _moe_plan(route[2].astype(jnp.int32), n, MOE_TILE)
    out = _moe(x2e, norm_ffn, w_exp_gate[0].astype(BF16), w_exp_up[0].astype(BF16),
               w_exp_down[0].astype(BF16), tile_elo, tile_ehi, tile_nv, src)
    return out.reshape(b, s, d)
```

```python
import functools

import jax
import jax.numpy as jnp
from jax import lax
from jax.experimental import pallas as pl
from jax.experimental.pallas import tpu as pltpu

F32 = jnp.float32
BF16 = jnp.bfloat16

D_MODEL = 1024
HEAD_DIM = 64
LANES = 128
BLOCK_Q = 128
ROPE_THETA = 10000.0
EPS = 1e-6
DILATIONS = (1, 4, 16)
DIL_WIDTH = 512
SB_WIDTH = 512
MEM_WIDTH = 512
MEM_HEADS = 4
MEM_HEAD_DIM = 128
MEM_LEN = 256
GATE_COLS = 3 * D_MODEL
A_COLS = 9 * DIL_WIDTH
B_COLS = 3 * SB_WIDTH
IN_COLS = A_COLS + B_COLS + MEM_WIDTH + GATE_COLS
CB_DIL = GATE_COLS // LANES
CB_SB = CB_DIL + A_COLS // LANES
CB_MEM = CB_SB + B_COLS // LANES
N_GROUPS = 4
EXPERTS_PER_GROUP = 4
N_EXPERTS = 16
D_EXPERT = 512
N_PAIRS = 6
N_CLASSES = N_GROUPS * N_PAIRS
MOE_TILE = 256
STAGE_GAP = 2
DMA_UNROLL = 8
X2_ROWS = 16
NEG = -1e30
VMEM_LIMIT = 56 * 1024 * 1024


def _cparams(sem):
    return pltpu.CompilerParams(dimension_semantics=sem, vmem_limit_bytes=VMEM_LIMIT)


def _dot(a, b):
    return jnp.dot(a, b, preferred_element_type=F32)


def _dot_nt(a, b):
    return lax.dot_general(a, b, (((1,), (1,)), ((), ())), preferred_element_type=F32)


def _split_bf16(a):
    hi = a.astype(BF16)
    return hi, (a - hi.astype(F32)).astype(BF16)


def _head_masks(rows):
    lane = lax.broadcasted_iota(jnp.int32, (rows, LANES), 1)
    return lane < HEAD_DIM


def _inproj_kernel(x_ref, g_ref, w_ref, o_ref, *, chunk):
    x = x_ref[...]
    ms = jnp.mean(x * x, axis=-1, keepdims=True)
    h = (x * lax.rsqrt(ms + EPS) * g_ref[...]).astype(BF16)
    for c in range(IN_COLS // chunk):
        sl = slice(c * chunk, (c + 1) * chunk)
        o_ref[:, sl] = _dot(h, w_ref[:, sl]).astype(BF16)


def _in_proj(x2d, gain, w_bf16, tm=256, chunk=512):
    n = x2d.shape[0]
    return pl.pallas_call(
        functools.partial(_inproj_kernel, chunk=chunk),
        out_shape=jax.ShapeDtypeStruct((n, IN_COLS), BF16),
        grid=(n // tm,),
        in_specs=[
            pl.BlockSpec((tm, D_MODEL), lambda i: (i, 0)),
            pl.BlockSpec((1, D_MODEL), lambda i: (0, 0)),
            pl.BlockSpec((D_MODEL, IN_COLS), lambda i: (0, 0), pipeline_mode=pl.Buffered(1)),
        ],
        out_specs=pl.BlockSpec((tm, IN_COLS), lambda i: (i, 0)),
        compiler_params=_cparams(("parallel",)),
        name="in_proj",
    )(x2d, gain, w_bf16)


def _rope_kernel(pos_ref, invf_ref, sign_ref, cos_ref, sin_ref):
    ang = pos_ref[...] * invf_ref[...]
    cos_ref[...] = jnp.cos(ang)
    sin_ref[...] = jnp.sin(ang) * sign_ref[...]


def _rope_tables(pos_b, invf, sign, tm=2048):
    n = pos_b.shape[0]
    row = pl.BlockSpec((tm, LANES), lambda i: (i, 0))
    vec = pl.BlockSpec((1, LANES), lambda i: (0, 0))
    return pl.pallas_call(
        _rope_kernel,
        out_shape=(jax.ShapeDtypeStruct((n, LANES), F32),) * 2,
        grid=(n // tm,),
        in_specs=[row, vec, vec],
        out_specs=(row, row),
        compiler_params=_cparams(("parallel",)),
        name="rope_tab",
    )(pos_b, invf, sign)


def _dil_kernel(q0, k0, v0, q1, k1, v1, q2, k2, v2, cos_ref, sin_ref, qn_ref, kn_ref, o_ref,
                qp, km0, km1, vm0, vm1, st, stq, stk, stv, stg, *, seq):
    qkv = ((q0, k0, v0), (q1, k1, v1), (q2, k2, v2))
    chunk = 2 * BLOCK_Q
    n_chunks = seq // chunk

    gi = lax.broadcasted_iota(jnp.int32, (LANES, LANES), 0) // HEAD_DIM
    gj = lax.broadcasted_iota(jnp.int32, (LANES, LANES), 1) // HEAD_DIM
    head_mean = jnp.where(gi == gj, 1.0 / HEAD_DIM, 0.0).astype(BF16)
    lane_c = lax.broadcasted_iota(jnp.int32, (chunk, LANES), 1)
    first_half = (lane_c % HEAD_DIM) < (HEAD_DIM // 2)

    def norm_rope(t_ref, gain, rows, scale):
        t = t_ref[0, rows, :].astype(F32)
        hi, lo = _split_bf16(t * t)
        tn = t * lax.rsqrt(_dot(hi, head_mean) + _dot(lo, head_mean) + EPS) * gain
        rot = jnp.where(first_half, pltpu.roll(tn, LANES - HEAD_DIM // 2, 1), pltpu.roll(tn, HEAD_DIM // 2, 1))
        out = tn * cos_ref[0, rows, :] + rot * sin_ref[0, rows, :]
        return out * scale if scale != 1.0 else out

    def put(g, dst, qb, kb, vb):
        h = _head_masks(kb.shape[0])
        zero = jnp.zeros_like(kb)
        qp[g, dst, :] = qb
        km0[g, dst, :] = jnp.where(h, kb, zero)
        km1[g, dst, :] = jnp.where(h, zero, kb)
        vm0[g, dst, :] = jnp.where(h, vb, zero)
        vm1[g, dst, :] = jnp.where(h, zero, vb)

    def prep(i, _):
        rows = pl.ds(pl.multiple_of(i * chunk, chunk), chunk)
        for g, dil in enumerate(DILATIONS):
            q_ref, k_ref, v_ref = qkv[g]
            qn = norm_rope(q_ref, qn_ref[...], rows, HEAD_DIM ** -0.5)
            kn = norm_rope(k_ref, kn_ref[...], rows, 1.0)
            if dil == 1:
                put(g, rows, qn.astype(BF16), kn.astype(BF16), v_ref[0, rows, :])
                continue
            stq[g - 1] = qn
            stk[g - 1] = kn
            stv[g - 1] = v_ref[0, rows, :].astype(F32)
            piece, sub = chunk // dil, seq // dil
            for c in range(dil):
                src = pl.ds(c, piece, stride=dil)
                dst = pl.ds(pl.multiple_of(c * sub + i * piece, piece), piece)
                put(g, dst, stq[g - 1, src, :].astype(BF16), stk[g - 1, src, :].astype(BF16),
                    stv[g - 1, src, :].astype(BF16))
        return 0

    lax.fori_loop(0, n_chunks, prep, 0, unroll=2)

    qi = lax.broadcasted_iota(jnp.int32, (BLOCK_Q, BLOCK_Q), 0)
    kj = lax.broadcasted_iota(jnp.int32, (BLOCK_Q, BLOCK_Q), 1)
    bias_cur = jnp.where(kj <= qi, 0.0, NEG).astype(F32)
    bias_prev = jnp.where(kj >= qi, 0.0, NEG).astype(F32)
    bias = {BLOCK_Q: jnp.concatenate([bias_cur] * 2, axis=1),
            2 * BLOCK_Q: jnp.concatenate([bias_prev, bias_cur] * 2, axis=1)}
    hq = _head_masks(BLOCK_Q)
    ones = {nk: jnp.where(_head_masks(nk), 1.0, 0.0).astype(BF16) for nk in bias}

    def scores(g, q_lo, k_lo, nk):
        keys = jnp.concatenate([km0[g, k_lo:k_lo + nk, :], km1[g, k_lo:k_lo + nk, :]], axis=0)
        return _dot_nt(qp[g, q_lo:q_lo + BLOCK_Q, :], keys)

    def softmax(s, nk):
        s = s + bias[nk]
        mx = [jnp.max(s[:, h * nk:(h + 1) * nk], axis=-1, keepdims=True) for h in range(2)]
        m_all = jnp.concatenate([jnp.broadcast_to(m, (BLOCK_Q, nk)) for m in mx], axis=1)
        return jnp.exp(s - m_all).astype(BF16), jnp.where(hq, mx[0], mx[1])

    def weighted(p, mt, g, q_lo, k_lo, nk):
        rhs = jnp.concatenate([
            jnp.concatenate([vm0[g, k_lo:k_lo + nk, :], ones[nk]], axis=1),
            jnp.concatenate([vm1[g, k_lo:k_lo + nk, :], 1.0 - ones[nk]], axis=1)], axis=0)
        pv = _dot(p, rhs)
        st[g, 0, q_lo:q_lo + BLOCK_Q, :] = pv[:, :LANES]
        st[g, 1, q_lo:q_lo + BLOCK_Q, :] = pv[:, LANES:]
        st[g, 2, q_lo:q_lo + BLOCK_Q, :] = mt

    blocks = []
    for g, dil in enumerate(DILATIONS):
        sub = seq // dil
        for c in range(dil):
            for n in range(sub // BLOCK_Q):
                q_lo = c * sub + n * BLOCK_Q
                blocks.append((g, q_lo, q_lo - BLOCK_Q, 2 * BLOCK_Q) if n else (g, q_lo, q_lo, BLOCK_Q))
    ss, ps = {}, {}
    for i in range(len(blocks) + 2 * STAGE_GAP):
        if i >= 2 * STAGE_GAP:
            j = i - 2 * STAGE_GAP
            weighted(*ps.pop(j), *blocks[j])
        if STAGE_GAP <= i < len(blocks) + STAGE_GAP:
            j = i - STAGE_GAP
            ps[j] = softmax(ss.pop(j), blocks[j][3])
        if i < len(blocks):
            ss[i] = scores(*blocks[i])

    def merge(i, _):
        rows = pl.ds(pl.multiple_of(i * chunk, chunk), chunk)
        parts = []
        for g, dil in enumerate(DILATIONS):
            if dil == 1:
                parts.append([st[g, k, rows, :] for k in range(3)])
                continue
            piece, sub = chunk // dil, seq // dil
            for c in range(dil):
                src = pl.ds(pl.multiple_of(c * sub + i * piece, piece), piece)
                dst = pl.ds(c, piece, stride=dil)
                for k in range(3):
                    stg[g - 1, k, dst, :] = st[g, k, src, :]
            parts.append([stg[g - 1, k] for k in range(3)])
        m_max = functools.reduce(jnp.maximum, [m for _, _, m in parts])
        num = den = None
        for pv, rowsum, m in parts:
            w = jnp.exp(m - m_max)
            num = w * pv if num is None else num + w * pv
            den = w * rowsum if den is None else den + w * rowsum
        o_ref[0, rows, :] = (num / den).astype(BF16)
        return 0

    lax.fori_loop(0, n_chunks, merge, 0)


def _dilated(proj3, cos3, sin3, qn2, kn2):
    b, s, _ = proj3.shape
    specs = []
    for g in range(3):
        for t in range(3):
            cb = CB_DIL + (t * 3 + g) * 4
            specs.append(pl.BlockSpec((1, s, LANES), lambda bi, j, cb=cb: (bi, 0, cb + j)))
    tab = pl.BlockSpec((1, s, LANES), lambda bi, j: (bi, 0, 0))
    vec = pl.BlockSpec((1, LANES), lambda bi, j: (0, 0))
    chunk = 2 * BLOCK_Q
    return pl.pallas_call(
        functools.partial(_dil_kernel, seq=s),
        out_shape=jax.ShapeDtypeStruct((b, s, DIL_WIDTH), BF16),
        grid=(b, DIL_WIDTH // LANES),
        in_specs=specs + [tab, tab, vec, vec],
        out_specs=pl.BlockSpec((1, s, LANES), lambda bi, j: (bi, 0, j)),
        scratch_shapes=[pltpu.VMEM((3, s, LANES), BF16)] * 5
        + [pltpu.VMEM((3, 3, s, LANES), F32)]
        + [pltpu.VMEM((2, chunk, LANES), F32)] * 3
        + [pltpu.VMEM((2, 3, chunk, LANES), F32)],
        compiler_params=_cparams(("parallel", "arbitrary")),
        name="dilated",
    )(*([proj3] * 9), cos3, sin3, qn2, kn2)


def _sb_kernel(q_ref, k_ref, v_ref, o_ref, qs, ks, vs, *, seq):
    nb = seq // BLOCK_Q
    qi = lax.broadcasted_iota(jnp.int32, (BLOCK_Q, 2 * BLOCK_Q), 0)
    kj = lax.broadcasted_iota(jnp.int32, (BLOCK_Q, 2 * BLOCK_Q), 1) % BLOCK_Q
    causal = kj < qi
    uj = lax.broadcasted_iota(jnp.int32, (2 * BLOCK_Q, 2 * BLOCK_Q), 0) % BLOCK_Q
    us = lax.broadcasted_iota(jnp.int32, (2 * BLOCK_Q, 2 * BLOCK_Q), 1)
    suffix = jnp.where((uj > us) | (us >= BLOCK_Q), 1.0, 0.0).astype(BF16)
    h0 = _head_masks(BLOCK_Q)

    def prep(i, _):
        rows = pl.ds(pl.multiple_of(i * BLOCK_Q, BLOCK_Q), BLOCK_Q)
        qs[rows, :] = q_ref[0, rows, :] * jnp.asarray(HEAD_DIM ** -0.5, BF16)
        k = k_ref[0, rows, :]
        v = v_ref[0, rows, :]
        zero = jnp.zeros_like(k)
        ks[i, :BLOCK_Q, :] = jnp.where(h0, k, zero)
        ks[i, BLOCK_Q:, :] = jnp.where(h0, zero, k)
        vs[i, :BLOCK_Q, :] = jnp.where(h0, v, zero)
        vs[i, BLOCK_Q:, :] = jnp.where(h0, zero, v)
        return 0

    lax.fori_loop(0, nb, prep, 0)

    def scores(rb, kb):
        return _dot_nt(qs[pl.ds(rb * BLOCK_Q, BLOCK_Q), :], ks[kb])

    def log_weights(z, diag):
        neg = jnp.minimum(z, 0.0)
        sp = jnp.log(1.0 + jnp.exp(-jnp.abs(z)))
        log_beta = neg - sp
        log_1m = (neg - z) - sp
        if diag:
            log_1m = jnp.where(causal, log_1m, 0.0)
        hi, lo = _split_bf16(log_1m)
        lhs = jnp.concatenate([jnp.concatenate([hi[:, sl], lo[:, sl]], axis=1)
                               for sl in (slice(0, BLOCK_Q), slice(BLOCK_Q, 2 * BLOCK_Q))], axis=0)
        su = _dot(lhs, suffix)
        w = log_beta + jnp.concatenate([su[:BLOCK_Q, :BLOCK_Q], su[BLOCK_Q:, :BLOCK_Q]], axis=1)
        tot = jnp.concatenate([su[:BLOCK_Q, BLOCK_Q:], su[BLOCK_Q:, BLOCK_Q:]], axis=1)
        return w, tot

    def accumulate(state, w, tot, rb, kb):
        diag = kb == rb
        c, acc = (None, None) if diag else state
        a = jnp.exp(w if diag else w + c)
        if diag:
            a = jnp.where(causal, a, 0.0)
        pv = _dot(a.astype(BF16), vs[kb])
        acc = pv if diag else acc + pv
        if kb == 0:
            o_ref[0, pl.ds(rb * BLOCK_Q, BLOCK_Q), :] = acc.astype(BF16)
        elif diag:
            c = tot
        else:
            c = c + tot
        return c, acc

    pairs = [(rb, kb) for rb in range(nb) for kb in range(rb, -1, -1)]
    zs, wts, state = {}, {}, None
    for i in range(len(pairs) + 2 * STAGE_GAP):
        if i >= 2 * STAGE_GAP:
            j = i - 2 * STAGE_GAP
            state = accumulate(state, *wts.pop(j), *pairs[j])
        if STAGE_GAP <= i < len(pairs) + STAGE_GAP:
            j = i - STAGE_GAP
            wts[j] = log_weights(zs.pop(j), pairs[j][0] == pairs[j][1])
        if i < len(pairs):
            zs[i] = scores(*pairs[i])


def _stick_breaking(proj3):
    b, s, _ = proj3.shape
    nb = s // BLOCK_Q
    specs = [pl.BlockSpec((1, s, LANES), lambda bi, j, cb=CB_SB + t * 4: (bi, 0, cb + j)) for t in range(3)]
    return pl.pallas_call(
        functools.partial(_sb_kernel, seq=s),
        out_shape=jax.ShapeDtypeStruct((b, s, SB_WIDTH), BF16),
        grid=(b, SB_WIDTH // LANES),
        in_specs=specs,
        out_specs=pl.BlockSpec((1, s, LANES), lambda bi, j: (bi, 0, j)),
        scratch_shapes=[pltpu.VMEM((s, LANES), BF16)] + [pltpu.VMEM((nb, 2 * BLOCK_Q, LANES), BF16)] * 2,
        compiler_params=_cparams(("parallel", "arbitrary")),
        name="stickbrk",
    )(proj3, proj3, proj3)


def _memkv_kernel(mem_ref, g_ref, w_ref, kn_ref, k_ref, v_ref):
    x = mem_ref[0]
    ms = jnp.mean(x * x, axis=-1, keepdims=True)
    h = (x * lax.rsqrt(ms + EPS) * g_ref[...]).astype(BF16)
    kv = _dot(h, w_ref[...])
    for hd in range(MEM_HEADS):
        sl = slice(hd * MEM_HEAD_DIM, (hd + 1) * MEM_HEAD_DIM)
        kh = kv[:, sl]
        msk = jnp.mean(kh * kh, axis=-1, keepdims=True)
        k_ref[0, :, sl] = (kh * lax.rsqrt(msk + EPS) * kn_ref[...]).astype(BF16)
    v_ref[0] = kv[:, MEM_WIDTH:].astype(BF16)


def _mem_kv(mem, gain, w_bf16, kn):
    b = mem.shape[0]
    out = jax.ShapeDtypeStruct((b, MEM_LEN, MEM_WIDTH), BF16)
    blk = pl.BlockSpec((1, MEM_LEN, MEM_WIDTH), lambda i: (i, 0, 0))
    return pl.pallas_call(
        _memkv_kernel,
        out_shape=(out, out),
        grid=(b,),
        in_specs=[
            pl.BlockSpec((1, MEM_LEN, D_MODEL), lambda i: (i, 0, 0)),
            pl.BlockSpec((1, D_MODEL), lambda i: (0, 0)),
            pl.BlockSpec((D_MODEL, 2 * MEM_WIDTH), lambda i: (0, 0)),
            pl.BlockSpec((1, MEM_HEAD_DIM), lambda i: (0, 0)),
        ],
        out_specs=(blk, blk),
        compiler_params=_cparams(("parallel",)),
        name="mem_kv",
    )(mem, gain, w_bf16, kn)


def _memattn_kernel(q_ref, k_ref, v_ref, qn_ref, o_ref):
    q = q_ref[0].astype(F32)
    ms = jnp.mean(q * q, axis=-1, keepdims=True)
    qn = (q * lax.rsqrt(ms + EPS) * qn_ref[...]).astype(BF16)
    s = _dot_nt(qn, k_ref[0]) * (MEM_HEAD_DIM ** -0.5)
    m = jnp.max(s, axis=-1, keepdims=True)
    p = jnp.exp(s - m)
    den = jnp.sum(p, axis=-1, keepdims=True)
    o_ref[0] = (_dot(p.astype(BF16), v_ref[0]) / den).astype(BF16)


def _mem_attn(proj3, k, v, qn, tq=1024):
    b, s, _ = proj3.shape
    kv = pl.BlockSpec((1, MEM_LEN, MEM_HEAD_DIM), lambda bi, h, i: (bi, 0, h))
    return pl.pallas_call(
        _memattn_kernel,
        out_shape=jax.ShapeDtypeStruct((b, s, MEM_WIDTH), BF16),
        grid=(b, MEM_HEADS, s // tq),
        in_specs=[
            pl.BlockSpec((1, tq, MEM_HEAD_DIM), lambda bi, h, i: (bi, i, CB_MEM + h)),
            kv, kv,
            pl.BlockSpec((1, MEM_HEAD_DIM), lambda bi, h, i: (0, 0)),
        ],
        out_specs=pl.BlockSpec((1, tq, MEM_HEAD_DIM), lambda bi, h, i: (bi, i, h)),
        compiler_params=_cparams(("parallel", "parallel", "arbitrary")),
        name="mem_attn",
    )(proj3, k, v, qn)


def _merge_kernel(x_ref, gl_ref, bg_ref, od_ref, os_ref, om_ref, wd_ref, ws_ref, wm_ref, wo_ref,
                  nf_ref, wr_ref, br_ref, x2_ref, r_ref):
    merged = None
    for i, (o_ref, w_ref) in enumerate(((od_ref, wd_ref), (os_ref, ws_ref), (om_ref, wm_ref))):
        sl = slice(i * D_MODEL, (i + 1) * D_MODEL)
        gate = jax.nn.sigmoid(gl_ref[:, sl].astype(F32) + bg_ref[:, sl])
        term = gate * _dot(o_ref[...], w_ref[...])
        merged = term if merged is None else merged + term
    x2 = x_ref[...] + _dot(merged.astype(BF16), wo_ref[...])
    tm = x2.shape[0]
    for j in range(D_MODEL // LANES):
        x2_ref[pl.ds(j, tm, stride=X2_ROWS), :] = x2[:, j * LANES:(j + 1) * LANES]

    ms = jnp.mean(x2 * x2, axis=-1, keepdims=True)
    hn = x2 * lax.rsqrt(ms + EPS) * nf_ref[...]
    h_hi, h_lo = _split_bf16(hn)
    w_hi, w_lo = _split_bf16(wr_ref[...])
    lt = (_dot(h_hi, w_hi) + _dot(h_lo, w_hi) + _dot(h_hi, w_lo)).T + br_ref[...]
    row = lambda i: lt[i:i + 1, :]
    first_max = lambda v, mx: jnp.where(v[0] == mx, 0, jnp.where(v[1] == mx, 1, jnp.where(v[2] == mx, 2, 3)))
    gl = [row(i) for i in range(N_GROUPS)]
    gmax = functools.reduce(jnp.maximum, gl)
    g_top = 1.0 / functools.reduce(lambda a, c: a + c, [jnp.exp(v - gmax) for v in gl])
    g_idx = first_max(gl, gmax)
    el = []
    for e in range(EXPERTS_PER_GROUP):
        v = row(N_GROUPS + 3 * EXPERTS_PER_GROUP + e)
        for g in range(N_GROUPS - 2, -1, -1):
            v = jnp.where(g_idx == g, row(N_GROUPS + g * EXPERTS_PER_GROUP + e), v)
        el.append(v)
    emax = functools.reduce(jnp.maximum, el)
    a_idx = first_max(el, emax)
    rest = [jnp.where(a_idx == e, -jnp.inf, el[e]) for e in range(EXPERTS_PER_GROUP)]
    rmax = functools.reduce(jnp.maximum, rest)
    b_idx = first_max(rest, rmax)
    ratio = jnp.exp(rmax - emax)
    w_a = g_top / (1.0 + ratio)
    w_b = g_top * ratio / (1.0 + ratio)
    lo = jnp.minimum(a_idx, b_idx)
    hi = jnp.maximum(a_idx, b_idx)
    pair = jnp.where(lo == 0, hi - 1, jnp.where(lo == 1, hi + 1, 5))
    cls = g_idx * N_PAIRS + pair
    a_first = a_idx < b_idx
    tm = lt.shape[1]
    ri = lax.broadcasted_iota(jnp.int32, (8, tm), 0)
    info = jnp.where(ri == 0, jnp.where(a_first, w_a, w_b),
                     jnp.where(ri == 1, jnp.where(a_first, w_b, w_a), jnp.where(ri == 2, cls.astype(F32), 0.0)))
    r_ref[...] = info
    x2_ref[pl.ds(D_MODEL // LANES, tm, stride=X2_ROWS), :] = jnp.concatenate(
        [info, jnp.zeros((LANES - 8, tm), F32)], axis=0).T
    for j in range(D_MODEL // LANES + 1, X2_ROWS):
        x2_ref[pl.ds(j, tm, stride=X2_ROWS), :] = jnp.zeros((tm, LANES), F32)


def _merge(x2d, proj, b_gate, o_dil, o_sb, o_mem, w_o_dil, w_o_sb, w_o_mem, w_out, norm_ffn, w_rt, b_rt, tm=512):
    n = x2d.shape[0]
    row = lambda w: pl.BlockSpec((tm, w), lambda i: (i, 0))
    full = lambda a, bdim: pl.BlockSpec((a, bdim), lambda i: (0, 0))
    return pl.pallas_call(
        _merge_kernel,
        out_shape=(jax.ShapeDtypeStruct((n * X2_ROWS, LANES), F32), jax.ShapeDtypeStruct((8, n), F32)),
        grid=(n // tm,),
        in_specs=[
            row(D_MODEL), row(GATE_COLS), full(1, GATE_COLS),
            row(DIL_WIDTH), row(SB_WIDTH), row(MEM_WIDTH),
            full(DIL_WIDTH, D_MODEL), full(SB_WIDTH, D_MODEL), full(MEM_WIDTH, D_MODEL), full(D_MODEL, D_MODEL),
            full(1, D_MODEL), full(D_MODEL, LANES), full(LANES, 1),
        ],
        out_specs=(pl.BlockSpec((tm * X2_ROWS, LANES), lambda i: (i, 0)), pl.BlockSpec((8, tm), lambda i: (0, i))),
        compiler_params=_cparams(("parallel",)),
        name="merge",
    )(x2d, proj, b_gate, o_dil, o_sb, o_mem, w_o_dil, w_o_sb, w_o_mem, w_out, norm_ffn, w_rt, b_rt)


def _moe_kernel(elo_ref, ehi_ref, nv_ref, src_ref,
                x_hbm, nf_ref, wgl_ref, wul_ref, wdl_ref, wgh_ref, wuh_ref, wdh_ref,
                out_hbm, xbuf, obuf, gsem, ssem, *, tile, n_tiles):
    i = pl.program_id(0)
    slot = i % 2

    def row_in(tok, r, s):
        return pltpu.make_async_copy(x_hbm.at[pl.ds(pl.multiple_of(tok * X2_ROWS, X2_ROWS), X2_ROWS), :],
                                     xbuf.at[s, pl.ds(r * X2_ROWS, X2_ROWS), :], gsem.at[s])

    def row_out(tok, r, s):
        return pltpu.make_async_copy(obuf.at[s, pl.ds(r, 1), :], out_hbm.at[pl.ds(tok, 1), :], ssem.at[s])

    def rows_out(t, s, count, wait):
        def one(r, priority=0):
            if wait:
                row_out(0, r, s).wait()
            else:
                row_out(src_ref[t * tile + r], r, s).start(priority=priority)

        def chunk(c, _):
            for u in range(DMA_UNROLL):
                one(c * DMA_UNROLL + u, u % 2)
            return 0

        def single(r, _):
            one(r)
            return 0

        full = count // DMA_UNROLL
        lax.fori_loop(0, full, chunk, 0)
        lax.fori_loop(full * DMA_UNROLL, count, single, 0)

    def drain_out(t, s):
        rows_out(t, s, nv_ref[t], True)

    def wait_gather():
        for r in range(tile):
            row_in(0, r, slot).wait()

    nv = nv_ref[i]

    @pl.when((i == 0) & (nv > 0))
    def _():
        def chunk(c, _):
            for u in range(DMA_UNROLL):
                r = c * DMA_UNROLL + u
                row_in(src_ref[r], r, 0).start()
            return 0
        lax.fori_loop(0, tile // DMA_UNROLL, chunk, 0)

    @pl.when(i >= 2)
    def _():
        drain_out(i - 2, slot)

    prev_nv = jnp.where(i > 0, nv_ref[jnp.maximum(i - 1, 0)], 0)
    prev_inline = (nv > 0) & (prev_nv == tile)

    @pl.when((prev_nv > 0) & jnp.logical_not(prev_inline))
    def _():
        rows_out(i - 1, 1 - slot, prev_nv, False)

    @pl.when((nv == 0) & (prev_nv > 0))
    def _():
        wait_gather()

    def expert_block(scatter_prev):
        wait_gather()
        for r in range(tile):
            row_in(src_ref[(i + 1) * tile + r], r, 1 - slot).start()
        if scatter_prev:
            for r in range(tile):
                row_out(src_ref[(i - 1) * tile + r], r, 1 - slot).start(priority=r % 2)

        x = jnp.concatenate([xbuf[slot, pl.ds(j, tile, stride=X2_ROWS), :] for j in range(D_MODEL // LANES)], axis=1)
        cw = xbuf[slot, pl.ds(D_MODEL // LANES, tile, stride=X2_ROWS), :]
        ms = jnp.mean(x * x, axis=-1, keepdims=True)
        hn = (x * lax.rsqrt(ms + EPS) * nf_ref[...]).astype(BF16)
        moe = None
        for c, (wg, wu, wd) in enumerate(((wgl_ref, wul_ref, wdl_ref), (wgh_ref, wuh_ref, wdh_ref))):
            gate = _dot(hn, wg[0])
            up = _dot(hn, wu[0])
            act = (gate * jax.nn.sigmoid(gate) * up).astype(BF16)
            term = cw[:, c:c + 1] * _dot(act, wd[0])
            moe = term if moe is None else moe + term
        obuf[slot] = x + moe

    @pl.when(prev_inline)
    def _():
        expert_block(True)

    @pl.when((nv > 0) & (prev_nv != tile))
    def _():
        expert_block(False)

    @pl.when((i == n_tiles - 1) & (i >= 1))
    def _():
        drain_out(i - 1, 1 - slot)


def _moe(x2e, norm_ffn, wg, wu, wd, tile_elo, tile_ehi, tile_nv, src, tile=MOE_TILE):
    n = x2e.shape[0] // X2_ROWS
    n_tiles = tile_nv.shape[0]
    w_in = lambda sel: pl.BlockSpec((1, D_MODEL, D_EXPERT), lambda i, elo, ehi, nv, s: ((elo, ehi)[sel][i], 0, 0))
    w_dn = lambda sel: pl.BlockSpec((1, D_EXPERT, D_MODEL), lambda i, elo, ehi, nv, s: ((elo, ehi)[sel][i], 0, 0))
    grid_spec = pltpu.PrefetchScalarGridSpec(
        num_scalar_prefetch=4,
        grid=(n_tiles,),
        in_specs=[
            pl.BlockSpec(memory_space=pl.ANY),
            pl.BlockSpec((1, D_MODEL), lambda i, *_: (0, 0)),
            w_in(0), w_in(0), w_dn(0), w_in(1), w_in(1), w_dn(1),
        ],
        out_specs=pl.BlockSpec(memory_space=pl.ANY),
        scratch_shapes=[
            pltpu.VMEM((2, tile * X2_ROWS, LANES), F32),
            pltpu.VMEM((2, tile, D_MODEL), F32),
            pltpu.SemaphoreType.DMA((2,)),
            pltpu.SemaphoreType.DMA((2,)),
        ],
    )
    return pl.pallas_call(
        functools.partial(_moe_kernel, tile=tile, n_tiles=n_tiles),
        out_shape=jax.ShapeDtypeStruct((n, D_MODEL), F32),
        grid_spec=grid_spec,
        compiler_params=_cparams(("arbitrary",)),
        name="moe",
    )(tile_elo, tile_ehi, tile_nv, src, x2e, norm_ffn, wg, wu, wd, wg, wu, wd)


def _moe_plan(cls, n, tile):
    n_tiles = (n + N_CLASSES * (tile - 1)) // tile + 1
    order = jnp.argsort(cls, stable=True).astype(jnp.int32)
    counts = jnp.sum((cls[:, None] == jnp.arange(N_CLASSES, dtype=jnp.int32)[None, :]).astype(jnp.int32), axis=0)
    tiles_per = (counts + tile - 1) // tile
    tile_end = jnp.cumsum(tiles_per)
    tile_start = tile_end - tiles_per
    tok_start = jnp.cumsum(counts) - counts
    t = jnp.arange(n_tiles, dtype=jnp.int32)
    tile_cls = jnp.minimum(jnp.sum((t[:, None] >= tile_end[None, :]).astype(jnp.int32), axis=1), N_CLASSES - 1)
    rank0 = (t - tile_start[tile_cls]) * tile
    tile_nv = jnp.clip(counts[tile_cls] - rank0, 0, tile).astype(jnp.int32)
    tile_nv = jnp.where(t < tile_end[-1], tile_nv, 0)
    r = jnp.arange(tile, dtype=jnp.int32)
    pos = tok_start[tile_cls][:, None] + rank0[:, None] + r[None, :]
    valid = r[None, :] < tile_nv[:, None]
    src = jnp.where(valid, order[jnp.clip(pos, 0, n - 1)], 0).reshape(-1).astype(jnp.int32)
    pair_lo = jnp.array([0, 0, 0, 1, 1, 2], jnp.int32)
    pair_hi = jnp.array([1, 2, 3, 2, 3, 3], jnp.int32)
    grp = tile_cls // N_PAIRS
    tile_elo = (grp * EXPERTS_PER_GROUP + pair_lo[tile_cls % N_PAIRS]).astype(jnp.int32)
    tile_ehi = (grp * EXPERTS_PER_GROUP + pair_hi[tile_cls % N_PAIRS]).astype(jnp.int32)
    return tile_elo, tile_ehi, tile_nv, src


def kernel(x, mem, positions, norm_mix, norm_mem, w_in, b_gate, qn_dil, kn_dil, qn_mem, kn_mem, w_mem_kv,
           w_o_dil, w_o_sb, w_o_mem, w_out, norm_ffn, w_router_group, b_router_group, w_router_expert,
           b_router_expert, w_exp_gate, w_exp_up, w_exp_down):
    b, s, d = x.shape
    n = b * s
    assert d == D_MODEL and w_in.shape == (1, D_MODEL, IN_COLS) and s % (BLOCK_Q * DILATIONS[-1]) == 0
    off_g = IN_COLS - GATE_COLS
    x2d = x.reshape(n, d)

    w_in_p = jnp.concatenate([w_in[0][:, off_g:], w_in[0][:, :off_g]], axis=1).astype(BF16)
    proj = _in_proj(x2d, norm_mix, w_in_p)
    proj3 = proj.reshape(b, s, IN_COLS)

    inv_freq = ROPE_THETA ** (-jnp.arange(0, HEAD_DIM, 2, dtype=F32) / HEAD_DIM)
    invf = jnp.tile(inv_freq, LANES // (HEAD_DIM // 2))[None, :]
    sign = jnp.tile(jnp.concatenate([-jnp.ones(HEAD_DIM // 2, F32), jnp.ones(HEAD_DIM // 2, F32)]), 2)[None, :]
    pos_b = jnp.broadcast_to(positions.astype(F32).reshape(n, 1), (n, LANES))
    cos_t, sin_t = _rope_tables(pos_b, invf, sign)
    o_dil = _dilated(proj3, cos_t.reshape(b, s, LANES), sin_t.reshape(b, s, LANES),
                     jnp.tile(qn_dil, (1, 2)), jnp.tile(kn_dil, (1, 2)))

    o_sb = _stick_breaking(proj3)

    k_mem, v_mem = _mem_kv(mem, norm_mem, w_mem_kv[0].astype(BF16), kn_mem)
    o_mem = _mem_attn(proj3, k_mem, v_mem, qn_mem)

    pad = LANES - N_GROUPS - N_EXPERTS
    w_rt = jnp.concatenate([w_router_group[0], w_router_expert[0], jnp.zeros((D_MODEL, pad), F32)], axis=1)
    b_rt = jnp.concatenate([b_router_group[0], b_router_expert[0], jnp.zeros((pad,), F32)])[:, None]
    x2e, route = _merge(x2d, proj, b_gate, o_dil.reshape(n, DIL_WIDTH), o_sb.reshape(n, SB_WIDTH),
                        o_mem.reshape(n, MEM_WIDTH), w_o_dil[0].astype(BF16), w_o_sb[0].astype(BF16),
                        w_o_mem[0].astype(BF16), w_out[0].astype(BF16), norm_ffn, w_rt, b_rt)

    tile_elo, tile_ehi, tile_nv, src = _moe_plan(route[2].astype(jnp.int32), n, MOE_TILE)
    out = _moe(x2e, norm_ffn, w_exp_gate[0].astype(BF16), w_exp_up[0].astype(BF16),
               w_exp_down[0].astype(BF16), tile_elo, tile_ehi, tile_nv, src)
    return out.reshape(b, s, d)
```
